```python
import jax
import jax.numpy as jnp
from jax import lax
import numpy as np

D_MODEL = 1024
BATCH = 4
SEQ = 8192
DEPTH = 2

CHUNK = 64
EPS = 1e-6
CONV_CH = D_MODEL // 2
CONV_WIDTH = 31
GLA_HEADS = 4
GLA_DK = D_MODEL // 16
GLA_DV = D_MODEL // 8
GLA_GATE_RANK = 16
GLA_GATE_TAU = 16.0
FOX_HEADS = 16
FOX_HD = D_MODEL // FOX_HEADS
Q_BLOCK = 128
PEER_HEADS = 8
PEER_NKEYS = 128
PEER_N = PEER_NKEYS * PEER_NKEYS
PEER_QDIM = 256
PEER_TOPK = 16
PEER_TOKEN_BLOCK = 128

EVEN_IN = 2 * CONV_CH + 2 * GLA_HEADS * GLA_DK + 2 * GLA_HEADS * GLA_DV + GLA_GATE_RANK
ODD_IN = 3 * FOX_HEADS * FOX_HD + FOX_HEADS

kernel_name = 'hybrid_conv_gla_fox_peer_trunk'


def rms_norm(x, g):
    xf = x.astype(jnp.float32)
    y = xf * lax.rsqrt(jnp.mean(xf * xf, axis=-1, keepdims=True) + EPS)
    return (y * g.astype(jnp.float32)).astype(x.dtype)


def layer_norm(x, g, b):
    xf = x.astype(jnp.float32)
    mu = jnp.mean(xf, axis=-1, keepdims=True)
    xc = xf - mu
    y = xc * lax.rsqrt(jnp.mean(xc * xc, axis=-1, keepdims=True) + EPS)
    return (y * g.astype(jnp.float32) + b.astype(jnp.float32)).astype(x.dtype)


def conformer_conv(val, gate, conv_w, conv_b, ln_g, ln_b):
    u = val * jax.nn.sigmoid(gate)
    w = conv_w.astype(u.dtype)[:, None, :]
    y = lax.conv_general_dilated(u, w, window_strides=(1,), padding=[(CONV_WIDTH - 1, 0)],
                                 dimension_numbers=('NWC', 'WIO', 'NWC'),
                                 feature_group_count=u.shape[-1])
    y = y + conv_b.astype(y.dtype)
    return jax.nn.silu(layer_norm(y, ln_g, ln_b))


def gla_chunked(q, k, v, log_a):
    B, S, H, dk = q.shape
    dv = v.shape[-1]
    n = S // CHUNK

    def to_chunks(t):
        return t.reshape(B, n, CHUNK, H, t.shape[-1]).transpose(1, 0, 3, 2, 4)

    qc, kc, vc, gc = to_chunks(q * (dk ** -0.5)), to_chunks(k), to_chunks(v), to_chunks(log_a)
    causal = jnp.tril(jnp.ones((CHUNK, CHUNK), dtype=bool))

    def step(state, inp):
        qi, ki, vi, gi = inp
        b = jnp.cumsum(gi, axis=2)
        b_last = b[:, :, -1:, :]
        o_inter = jnp.einsum('bhtk,bhkv->bhtv', qi * jnp.exp(b), state)
        diff = b[:, :, :, None, :] - b[:, :, None, :, :]
        decay = jnp.exp(jnp.where(causal[:, :, None], diff, -jnp.inf))
        attn = jnp.einsum('bhtk,bhsk,bhtsk->bhts', qi, ki, decay)
        o = o_inter + jnp.einsum('bhts,bhsv->bhtv', attn, vi)
        k_dec = ki * jnp.exp(b_last - b)
        new_state = state * jnp.exp(b_last[:, :, 0, :, None]) + jnp.einsum('bhsk,bhsv->bhkv', k_dec, vi)
        return new_state, o

    s0 = jnp.zeros((B, H, dk, dv), jnp.float32)
    _, o = lax.scan(step, s0, (qc, kc, vc, gc))
    return o.transpose(1, 0, 3, 2, 4).reshape(B, S, H, dv)


def even_mix(h, w_in, conv_w, conv_b, ln_g, ln_b, gate_w2, gate_b, gla_norm_g, w_out):
    B, S, _ = h.shape
    hk = GLA_HEADS * GLA_DK
    hv = GLA_HEADS * GLA_DV
    p = h @ w_in
    cuts = [int(c) for c in np.cumsum([CONV_CH, CONV_CH, hk, hk, hv, hv])]
    a_val, a_gate, q, k, v, r, glr = jnp.split(p, cuts, axis=-1)
    y_a = conformer_conv(a_val, a_gate, conv_w, conv_b, ln_g, ln_b)
    f32 = jnp.float32
    log_a = jax.nn.log_sigmoid((glr @ gate_w2 + gate_b).astype(f32)) / GLA_GATE_TAU
    o = gla_chunked(q.reshape(B, S, GLA_HEADS, GLA_DK).astype(f32),
                    k.reshape(B, S, GLA_HEADS, GLA_DK).astype(f32),
                    v.reshape(B, S, GLA_HEADS, GLA_DV).astype(f32),
                    log_a.reshape(B, S, GLA_HEADS, GLA_DK))
    o = rms_norm(o, gla_norm_g).reshape(B, S, hv).astype(h.dtype)
    y_b = o * jax.nn.silu(r)
    return jnp.concatenate([y_a, y_b], axis=-1) @ w_out


def fox_mix(h, w_in, fgate_b, q_g, k_g, w_out):
    B, S, _ = h.shape
    hd = FOX_HEADS * FOX_HD
    p = h @ w_in
    q, k, v, fz = jnp.split(p, [hd, 2 * hd, 3 * hd], axis=-1)
    q = rms_norm(q.reshape(B, S, FOX_HEADS, FOX_HD), q_g).transpose(0, 2, 1, 3)
    k = rms_norm(k.reshape(B, S, FOX_HEADS, FOX_HD), k_g).transpose(0, 2, 1, 3)
    v = v.reshape(B, S, FOX_HEADS, FOX_HD).transpose(0, 2, 1, 3)
    f32 = jnp.float32
    log_f = jax.nn.log_sigmoid(fz.astype(f32) + fgate_b.astype(f32))
    c = jnp.cumsum(log_f, axis=1).transpose(0, 2, 1)
    scale = FOX_HD ** -0.5
    pos = jnp.arange(S)
    outs = []
    for i in range(S // Q_BLOCK):
        lo, hi = i * Q_BLOCK, (i + 1) * Q_BLOCK
        logits = jnp.einsum('bhqd,bhkd->bhqk', q[:, :, lo:hi], k[:, :, :hi]).astype(f32) * scale
        logits = logits + c[:, :, lo:hi, None] - c[:, :, None, :hi]
        mask = pos[lo:hi, None] >= pos[None, :hi]
        probs = jax.nn.softmax(jnp.where(mask, logits, -jnp.inf), axis=-1)
        outs.append(jnp.einsum('bhqk,bhkd->bhqd', probs.astype(v.dtype), v[:, :, :hi]))
    o = jnp.concatenate(outs, axis=2).transpose(0, 2, 1, 3).reshape(B, S, hd)
    return o @ w_out


def peer_ffn(h, wq, keys, u, v):
    B, S, D = h.shape
    f32 = jnp.float32
    half = PEER_QDIM // 2
    q = (h @ wq).reshape(B, S, PEER_HEADS, 2, half).astype(f32)
    s1 = jnp.einsum('bshd,hnd->bshn', q[..., 0, :], keys[:, 0].astype(f32))
    s2 = jnp.einsum('bshd,hnd->bshn', q[..., 1, :], keys[:, 1].astype(f32))
    v1, i1 = lax.top_k(s1, PEER_TOPK)
    v2, i2 = lax.top_k(s2, PEER_TOPK)
    cand = (v1[..., :, None] + v2[..., None, :]).reshape(B, S, PEER_HEADS, PEER_TOPK * PEER_TOPK)
    sc, j = lax.top_k(cand, PEER_TOPK)
    e1 = jnp.take_along_axis(i1, j // PEER_TOPK, axis=-1)
    e2 = jnp.take_along_axis(i2, j % PEER_TOPK, axis=-1)
    experts = e1 * PEER_NKEYS + e2
    gates = jax.nn.softmax(sc, axis=-1)
    nb = (B * S) // PEER_TOKEN_BLOCK
    ne = PEER_HEADS * PEER_TOPK
    hb = h.reshape(nb, PEER_TOKEN_BLOCK, D)
    eb = experts.reshape(nb, PEER_TOKEN_BLOCK, ne)
    gb = gates.reshape(nb, PEER_TOKEN_BLOCK, ne).astype(h.dtype)

    def block(args):
        ht, et, gt = args
        act = jax.nn.gelu(jnp.einsum('td,ted->te', ht, u[et]), approximate=False)
        return jnp.einsum('te,ted->td', gt * act, v[et])

    out = lax.map(block, (hb, eb, gb))
    return out.reshape(B, S, D)


def setup_inputs(seed: int = 0) -> dict:
    key = jax.random.key(seed)
    ks = iter(jax.random.split(key, 32))
    n_even = (DEPTH + 1) // 2
    n_odd = DEPTH // 2
    f32 = jnp.float32
    D = D_MODEL

    def nrm(shape, scale):
        return jax.random.normal(next(ks), shape, f32) * scale

    def gain(shape):
        return 1.0 + 0.05 * jax.random.normal(next(ks), shape, f32)

    return {
        'x': nrm((BATCH, SEQ, D), 1.0),
        'ev_norm_mix': gain((n_even, D)),
        'ev_w_in': nrm((n_even, D, EVEN_IN), D ** -0.5),
        'ev_conv_w': nrm((n_even, CONV_WIDTH, CONV_CH), CONV_WIDTH ** -0.5),
        'ev_conv_b': nrm((n_even, CONV_CH), 0.02),
        'ev_conv_ln_g': gain((n_even, CONV_CH)),
        'ev_conv_ln_b': nrm((n_even, CONV_CH), 0.02),
        'ev_gate_w2': nrm((n_even, GLA_GATE_RANK, GLA_HEADS * GLA_DK), GLA_GATE_RANK ** -0.5),
        'ev_gate_b': nrm((n_even, GLA_HEADS * GLA_DK), 0.1),
        'ev_gla_norm_g': gain((n_even, GLA_DV)),
        'ev_w_out': nrm((n_even, CONV_CH + GLA_HEADS * GLA_DV, D), (CONV_CH + GLA_HEADS * GLA_DV) ** -0.5),
        'od_norm_mix': gain((n_odd, D)),
        'od_w_in': nrm((n_odd, D, ODD_IN), D ** -0.5),
        'od_fgate_b': jax.random.uniform(next(ks), (n_odd, FOX_HEADS), f32, minval=1.0, maxval=4.0),
        'od_q_norm_g': gain((n_odd, FOX_HD)),
        'od_k_norm_g': gain((n_odd, FOX_HD)),
        'od_w_out': nrm((n_odd, FOX_HEADS * FOX_HD, D), (FOX_HEADS * FOX_HD) ** -0.5),
        'ffn_norm': gain((DEPTH, D)),
        'peer_wq': nrm((DEPTH, D, PEER_HEADS * PEER_QDIM), D ** -0.5),
        'peer_keys': nrm((DEPTH, PEER_HEADS, 2, PEER_NKEYS, PEER_QDIM // 2), (PEER_QDIM // 2) ** -0.5),
        'peer_u': nrm((DEPTH, PEER_N, D), D ** -0.5),
        'peer_v': nrm((DEPTH, PEER_N, D), (PEER_HEADS * PEER_TOPK) ** -0.5),
    }


def reference(x, ev_norm_mix, ev_w_in, ev_conv_w, ev_conv_b, ev_conv_ln_g, ev_conv_ln_b,
              ev_gate_w2, ev_gate_b, ev_gla_norm_g, ev_w_out,
              od_norm_mix, od_w_in, od_fgate_b, od_q_norm_g, od_k_norm_g, od_w_out,
              ffn_norm, peer_wq, peer_keys, peer_u, peer_v):
    for layer in range(DEPTH):
        j = layer // 2
        if layer % 2 == 0:
            h = rms_norm(x, ev_norm_mix[j])
            x = x + even_mix(h, ev_w_in[j], ev_conv_w[j], ev_conv_b[j], ev_conv_ln_g[j], ev_conv_ln_b[j],
                             ev_gate_w2[j], ev_gate_b[j], ev_gla_norm_g[j], ev_w_out[j])
        else:
            h = rms_norm(x, od_norm_mix[j])
            x = x + fox_mix(h, od_w_in[j], od_fgate_b[j], od_q_norm_g[j], od_k_norm_g[j], od_w_out[j])
        h = rms_norm(x, ffn_norm[layer])
        x = x + peer_ffn(h, peer_wq[layer], peer_keys[layer], peer_u[layer], peer_v[layer])
    return x
```

```python
import functools

import jax
import jax.numpy as jnp
from jax import lax
from jax.experimental import pallas as pl
from jax.experimental.pallas import tpu as pltpu

F32 = jnp.float32
BF16 = jnp.bfloat16
HIGHEST = lax.Precision.HIGHEST

EPS = 1e-6
D_MODEL = 1024
CHUNK = 64
CONV_CH = 512
CONV_WIDTH = 31
GLA_HEADS = 4
GLA_DK = 64
GLA_DV = 128
GLA_GATE_RANK = 16
GLA_GATE_TAU = 16.0
FOX_HEADS = 16
FOX_HD = 64
PEER_HEADS = 8
PEER_NKEYS = 128
PEER_TOPK = 16
PEER_HALF = 128

LANES = 128
VMEM_LIMIT = 56 * 1024 * 1024


def _cparams(sem):
    return pltpu.CompilerParams(dimension_semantics=sem, vmem_limit_bytes=VMEM_LIMIT)


def _rms(x, g):
    ms = jnp.mean(x * x, axis=-1, keepdims=True)
    return x * lax.rsqrt(ms + EPS) * g


def _log_sigmoid(z):
    return jnp.minimum(z, 0.0) - jnp.log(1.0 + jnp.exp(-jnp.abs(z)))


def _dot(a, b, **kw):
    return jnp.dot(a, b, preferred_element_type=F32, **kw)


def _dot_nt(a, b, **kw):
    return lax.dot_general(a, b, (((1,), (1,)), ((), ())), preferred_element_type=F32, **kw)


def _even_in_kernel(x_ref, g_ref, w_ref, wkt_ref, wgt_ref,
                    aval_ref, agate_ref, q_ref, v_ref, r_ref, glr_ref, kt_ref, glrt_ref):
    xn = _rms(x_ref[...], g_ref[...]).astype(BF16)
    off = 0
    for o_ref in (aval_ref, agate_ref, q_ref, v_ref, r_ref, glr_ref):
        n = o_ref.shape[-1]
        o_ref[...] = _dot(xn, w_ref[:, off:off + n]).astype(o_ref.dtype)
        off += n
    kt_ref[...] = _dot_nt(wkt_ref[...], xn)
    glrt_ref[...] = _dot_nt(wgt_ref[...], xn)


def _even_in(x2, g, w_in, tm):
    T, D = x2.shape
    hk = GLA_HEADS * GLA_DK
    hv = GLA_HEADS * GLA_DV
    c0, c1, c2, c3, c4, c5 = CONV_CH, 2 * CONV_CH, 2 * CONV_CH + hk, 2 * CONV_CH + 2 * hk, 2 * CONV_CH + 2 * hk + hv, 2 * CONV_CH + 2 * hk + 2 * hv
    wb = w_in.astype(BF16)
    w_tok = jnp.concatenate([wb[:, :c2], wb[:, c3:]], axis=1)
    wkt = wb[:, c2:c3].T
    wgt = wb[:, c5:].T
    widths = (CONV_CH, CONV_CH, hk, hv, hv, GLA_GATE_RANK)
    out_shape = [jax.ShapeDtypeStruct((T, n), F32) for n in widths]
    out_shape += [jax.ShapeDtypeStruct((hk, T), F32), jax.ShapeDtypeStruct((GLA_GATE_RANK, T), F32)]
    out_specs = [pl.BlockSpec((tm, n), lambda i: (i, 0)) for n in widths]
    out_specs += [pl.BlockSpec((hk, tm), lambda i: (0, i)), pl.BlockSpec((GLA_GATE_RANK, tm), lambda i: (0, i))]
    return pl.pallas_call(
        _even_in_kernel,
        grid=(T // tm,),
        in_specs=[
            pl.BlockSpec((tm, D), lambda i: (i, 0)),
            pl.BlockSpec((1, D), lambda i: (0, 0)),
            pl.BlockSpec(w_tok.shape, lambda i: (0, 0)),
            pl.BlockSpec(wkt.shape, lambda i: (0, 0)),
            pl.BlockSpec(wgt.shape, lambda i: (0, 0)),
        ],
        out_specs=out_specs,
        out_shape=out_shape,
        compiler_params=_cparams(("parallel",)),
        name="even_in",
    )(x2, g.reshape(1, D), w_tok, wkt, wgt)


CONV_HALO = 32
CONV_ROWS = 64


def _conv_kernel(val_ref, gate_ref, hval_ref, hgate_ref, w_ref, b_ref, lg_ref, lb_ref, o_ref, ext_ref, *, ts):
    first = pl.program_id(1) == 0
    u_halo = hval_ref[0] * jax.nn.sigmoid(hgate_ref[0])
    ext_ref[0:CONV_HALO, :] = jnp.where(first, 0.0, u_halo)
    ext_ref[CONV_HALO:CONV_HALO + ts, :] = val_ref[0] * jax.nn.sigmoid(gate_ref[0])
    lead = CONV_HALO - (CONV_WIDTH - 1)
    for r0 in range(0, ts, CONV_ROWS):
        acc = jnp.broadcast_to(b_ref[...], (CONV_ROWS, CONV_CH))
        for k in range(CONV_WIDTH):
            acc = acc + w_ref[k:k + 1, :] * ext_ref[r0 + lead + k:r0 + lead + k + CONV_ROWS, :]
        mu = jnp.mean(acc, axis=-1, keepdims=True)
        xc = acc - mu
        y = xc * lax.rsqrt(jnp.mean(xc * xc, axis=-1, keepdims=True) + EPS) * lg_ref[...] + lb_ref[...]
        o_ref[0, r0:r0 + CONV_ROWS, :] = (y * jax.nn.sigmoid(y)).astype(o_ref.dtype)


def _conformer_conv(val, gate, conv_w, conv_b, ln_g, ln_b, ts):
    B, S, C = val.shape
    hb = ts // CONV_HALO
    cur = pl.BlockSpec((1, ts, C), lambda b, i: (b, i, 0))
    halo = pl.BlockSpec((1, CONV_HALO, C), lambda b, i: (b, jnp.maximum(i * hb - 1, 0), 0))
    vec = pl.BlockSpec((1, C), lambda b, i: (0, 0))
    return pl.pallas_call(
        functools.partial(_conv_kernel, ts=ts),
        grid=(B, S // ts),
        in_specs=[cur, cur, halo, halo, pl.BlockSpec((CONV_WIDTH, C), lambda b, i: (0, 0)), vec, vec, vec],
        out_specs=pl.BlockSpec((1, ts, C), lambda b, i: (b, i, 0)),
        out_shape=jax.ShapeDtypeStruct((B, S, C), BF16),
        scratch_shapes=[pltpu.VMEM((CONV_HALO + ts, C), F32)],
        compiler_params=_cparams(("parallel", "parallel")),
        name="conformer_conv",
    )(val, gate, val, gate, conv_w, conv_b.reshape(1, C), ln_g.reshape(1, C), ln_b.reshape(1, C))


def _gla_kernel(q_ref, v_ref, r_ref, glr_ref, kt_ref, glrt_ref, w2_ref, w2t_ref, gb_ref, gbt_ref, ng_ref,
                o_ref, state_ref, *, ts):
    @pl.when(pl.program_id(1) == 0)
    def _():
        state_ref[...] = jnp.zeros_like(state_ref)

    L = CHUNK
    row = lax.broadcasted_iota(jnp.int32, (L, L), 0)
    col = lax.broadcasted_iota(jnp.int32, (L, L), 1)
    causal = row >= col
    tril = causal.astype(F32)
    triu = (row <= col).astype(F32)
    mid = L // 2 - 1
    scale = GLA_DK ** -0.5

    la = _log_sigmoid(_dot(glr_ref[0], w2_ref[...], precision=HIGHEST) + gb_ref[...]) * (1.0 / GLA_GATE_TAU)
    lat = _log_sigmoid(_dot(w2t_ref[...], glrt_ref[...], precision=HIGHEST) + gbt_ref[...]) * (1.0 / GLA_GATE_TAU)

    for c in range(ts // L):
        sl = slice(c * L, (c + 1) * L)
        b = _dot(tril, la[sl, :], precision=HIGHEST)
        bt = _dot(lat[:, sl], triu, precision=HIGHEST)
        qc = q_ref[0, sl, :] * scale
        q_in = (qc * jnp.exp(b)).astype(BF16)
        q_a = (qc * jnp.exp(b - b[mid:mid + 1, :])).astype(BF16)
        kt = kt_ref[:, sl]
        bt_last = bt[:, L - 1:L]
        k_a = (kt * jnp.exp(bt[:, mid:mid + 1] - bt)).astype(BF16)
        k_dec = (kt * jnp.exp(bt_last - bt)).astype(BF16)
        v_c = v_ref[0, sl, :].astype(BF16)
        r_c = r_ref[0, sl, :]
        for h in range(GLA_HEADS):
            ks = slice(h * GLA_DK, (h + 1) * GLA_DK)
            vs = slice(h * GLA_DV, (h + 1) * GLA_DV)
            state = state_ref[h]
            attn = jnp.where(causal, _dot(q_a[:, ks], k_a[ks, :]), 0.0)
            o = _dot(q_in[:, ks], state.astype(BF16)) + _dot(attn.astype(BF16), v_c[:, vs])
            state_ref[h] = state * jnp.exp(bt_last[ks, :]) + _dot(k_dec[ks, :], v_c[:, vs])
            y = _rms(o, ng_ref[...])
            rg = r_c[:, vs]
            o_ref[0, sl, vs] = (y * (rg * jax.nn.sigmoid(rg))).astype(o_ref.dtype)


def _gla(q, v, r, glr, kt, glrt, gate_w2, gate_b, norm_g, B, S, ts):
    hk = GLA_HEADS * GLA_DK
    hv = GLA_HEADS * GLA_DV
    nt = S // ts
    tok = lambda n: pl.BlockSpec((1, ts, n), lambda b, i: (b, i, 0))
    tr = lambda n: pl.BlockSpec((n, ts), lambda b, i: (0, b * nt + i))
    full = lambda shp: pl.BlockSpec(shp, lambda b, i: (0,) * len(shp))
    return pl.pallas_call(
        functools.partial(_gla_kernel, ts=ts),
        grid=(B, nt),
        in_specs=[tok(hk), tok(hv), tok(hv), tok(GLA_GATE_RANK), tr(hk), tr(GLA_GATE_RANK),
                  full((GLA_GATE_RANK, hk)), full((hk, GLA_GATE_RANK)), full((1, hk)), full((hk, 1)), full((1, GLA_DV))],
        out_specs=tok(hv),
        out_shape=jax.ShapeDtypeStruct((B, S, hv), BF16),
        scratch_shapes=[pltpu.VMEM((GLA_HEADS, GLA_DK, GLA_DV), F32)],
        compiler_params=_cparams(("parallel", "arbitrary")),
        name="gla",
    )(q.reshape(B, S, hk), v.reshape(B, S, hv), r.reshape(B, S, hv), glr.reshape(B, S, GLA_GATE_RANK), kt, glrt,
      gate_w2, gate_w2.T, gate_b.reshape(1, hk), gate_b.reshape(hk, 1), norm_g.reshape(1, GLA_DV))


def _out2_kernel(x_ref, a_ref, b_ref, wa_ref, wb_ref, o_ref):
    o_ref[...] = x_ref[...] + _dot(a_ref[...], wa_ref[...]) + _dot(b_ref[...], wb_ref[...])


def _out_proj2(x2, a, b, w_out, tm):
    T, D = x2.shape
    ka, kb = a.shape[1], b.shape[1]
    wb16 = w_out.astype(BF16)
    return pl.pallas_call(
        _out2_kernel,
        grid=(T // tm,),
        in_specs=[pl.BlockSpec((tm, D), lambda i: (i, 0)), pl.BlockSpec((tm, ka), lambda i: (i, 0)),
                  pl.BlockSpec((tm, kb), lambda i: (i, 0)), pl.BlockSpec((ka, D), lambda i: (0, 0)),
                  pl.BlockSpec((kb, D), lambda i: (0, 0))],
        out_specs=pl.BlockSpec((tm, D), lambda i: (i, 0)),
        out_shape=jax.ShapeDtypeStruct((T, D), F32),
        compiler_params=_cparams(("parallel",)),
        name="out_proj2",
    )(x2, a, b, wb16[:ka], wb16[ka:])


def _out1_kernel(x_ref, a_ref, w_ref, o_ref):
    o_ref[...] = x_ref[...] + _dot(a_ref[...], w_ref[...])


def _out_proj1(x2, a, w_out, tm):
    T, D = x2.shape
    ka = a.shape[1]
    return pl.pallas_call(
        _out1_kernel,
        grid=(T // tm,),
        in_specs=[pl.BlockSpec((tm, D), lambda i: (i, 0)), pl.BlockSpec((tm, ka), lambda i: (i, 0)),
                  pl.BlockSpec((ka, D), lambda i: (0, 0))],
        out_specs=pl.BlockSpec((tm, D), lambda i: (i, 0)),
        out_shape=jax.ShapeDtypeStruct((T, D), F32),
        compiler_params=_cparams(("parallel",)),
        name="out_proj1",
    )(x2, a, w_out.astype(BF16))


def _even_mix(x2, B, S, norm_g, w_in, conv_w, conv_b, ln_g, ln_b, gate_w2, gate_b, gla_norm_g, w_out):
    a_val, a_gate, q, v, r, glr, kt, glrt = _even_in(x2, norm_g, w_in, tm=512)
    y_a = _conformer_conv(a_val.reshape(B, S, CONV_CH), a_gate.reshape(B, S, CONV_CH), conv_w, conv_b, ln_g, ln_b, ts=512)
    y_b = _gla(q, v, r, glr, kt, glrt, gate_w2, gate_b, gla_norm_g, B, S, ts=256)
    return _out_proj2(x2, y_a.reshape(B * S, CONV_CH), y_b.reshape(B * S, GLA_HEADS * GLA_DV), w_out, tm=512)


def _fox_in_kernel(x_ref, g_ref, w_ref, wft_ref, fb_ref, qg_ref, kg_ref, q_ref, k_ref, v_ref, c_ref, carry_ref, *, ts):
    @pl.when(pl.program_id(1) == 0)
    def _():
        carry_ref[...] = jnp.zeros_like(carry_ref)

    xn = _rms(x_ref[0], g_ref[...]).astype(BF16)
    hd = FOX_HEADS * FOX_HD
    q = _dot(xn, w_ref[:, 0:hd])
    k = _dot(xn, w_ref[:, hd:2 * hd])
    v = _dot(xn, w_ref[:, 2 * hd:3 * hd])
    scale = FOX_HD ** -0.5
    for h in range(FOX_HEADS):
        hs = slice(h * FOX_HD, (h + 1) * FOX_HD)
        q_ref[0, h] = (_rms(q[:, hs], qg_ref[...]) * scale).astype(q_ref.dtype)
        k_ref[0, h] = _rms(k[:, hs], kg_ref[...]).astype(k_ref.dtype)
        v_ref[0, h] = v[:, hs].astype(v_ref.dtype)
    lf = _log_sigmoid(_dot_nt(wft_ref[...], xn) + fb_ref[...])
    row = lax.broadcasted_iota(jnp.int32, (ts, ts), 0)
    col = lax.broadcasted_iota(jnp.int32, (ts, ts), 1)
    c = _dot(lf, (row <= col).astype(F32), precision=HIGHEST) + carry_ref[...]
    c_ref[0] = c
    carry_ref[...] = c[:, ts - 1:ts]


def _fox_in(x2, B, S, g, w_in, fgate_b, q_g, k_g, ts):
    D = x2.shape[1]
    hd = FOX_HEADS * FOX_HD
    wb = w_in.astype(BF16)
    w_tok = wb[:, :3 * hd]
    wft = wb[:, 3 * hd:].T
    head_shape = jax.ShapeDtypeStruct((B, FOX_HEADS, S, FOX_HD), BF16)
    head_spec = pl.BlockSpec((1, FOX_HEADS, ts, FOX_HD), lambda b, i: (b, 0, i, 0))
    full = lambda shp: pl.BlockSpec(shp, lambda b, i: (0,) * len(shp))
    return pl.pallas_call(
        functools.partial(_fox_in_kernel, ts=ts),
        grid=(B, S // ts),
        in_specs=[pl.BlockSpec((1, ts, D), lambda b, i: (b, i, 0)), full((1, D)), full(w_tok.shape), full(wft.shape),
                  full((FOX_HEADS, 1)), full((1, FOX_HD)), full((1, FOX_HD))],
        out_specs=[head_spec, head_spec, head_spec, pl.BlockSpec((1, FOX_HEADS, ts), lambda b, i: (b, 0, i))],
        out_shape=[head_shape, head_shape, head_shape, jax.ShapeDtypeStruct((B, FOX_HEADS, S), F32)],
        scratch_shapes=[pltpu.VMEM((FOX_HEADS, 1), F32)],
        compiler_params=_cparams(("parallel", "arbitrary")),
        name="fox_in",
    )(x2.reshape(B, S, D), g.reshape(1, D), w_tok, wft, fgate_b.reshape(FOX_HEADS, 1), q_g.reshape(1, FOX_HD),
      k_g.reshape(1, FOX_HD))


FOX_PAIR = 2


def _fox_attn_kernel(q_ref, k_ref, v_ref, c_ref, o_ref, *, tq):
    qi = pl.program_id(2)
    row = lax.broadcasted_iota(jnp.int32, (tq, tq), 0)
    col = lax.broadcasted_iota(jnp.int32, (tq, tq), 1)
    causal = row >= col
    outs = []
    for hh in range(FOX_PAIR):
        q = q_ref[0, hh]

        def step(ki, carry, masked):
            m, l, acc = carry
            k0 = pl.multiple_of(ki * tq, tq)
            k = k_ref[0, hh, pl.ds(k0, tq), :]
            v = v_ref[0, hh, pl.ds(k0, tq), :]
            s = _dot_nt(q, k) - c_ref[0, 0, hh, pl.ds(ki, 1), :]
            if masked:
                s = jnp.where(causal, s, -jnp.inf)
            m_new = jnp.maximum(m, jnp.max(s, axis=-1, keepdims=True))
            p = jnp.exp(s - m_new)
            alpha = jnp.exp(m - m_new)
            l = alpha * l + jnp.sum(p, axis=-1, keepdims=True)
            acc = alpha * acc + _dot(p.astype(BF16), v)
            return m_new, l, acc

        init = (jnp.full((tq, 1), -jnp.inf, F32), jnp.zeros((tq, 1), F32), jnp.zeros((tq, FOX_HD), F32))
        carry = lax.fori_loop(0, qi, functools.partial(step, masked=False), init)
        _, l, acc = step(qi, carry, True)
        outs.append(acc / l)
    o_ref[0] = jnp.concatenate(outs, axis=-1).astype(o_ref.dtype)


def _fox_attn(q, k, v, c, B, S, tq):
    npair = FOX_HEADS // FOX_PAIR
    c5 = c.reshape(B, npair, FOX_PAIR, S // tq, tq)
    qspec = pl.BlockSpec((1, FOX_PAIR, tq, FOX_HD), lambda b, h, i: (b, h, i, 0))
    kvspec = pl.BlockSpec((1, FOX_PAIR, S, FOX_HD), lambda b, h, i: (b, h, 0, 0))
    return pl.pallas_call(
        functools.partial(_fox_attn_kernel, tq=tq),
        grid=(B, npair, S // tq),
        in_specs=[qspec, kvspec, kvspec,
                  pl.BlockSpec((1, 1, FOX_PAIR, S // tq, tq), lambda b, h, i: (b, h, 0, 0, 0))],
        out_specs=pl.BlockSpec((1, tq, FOX_PAIR * FOX_HD), lambda b, h, i: (b, i, h)),
        out_shape=jax.ShapeDtypeStruct((B, S, FOX_HEADS * FOX_HD), BF16),
        compiler_params=_cparams(("parallel", "parallel", "arbitrary")),
        name="fox_attn",
    )(q, k, v, c5)


def _fox_mix(x2, B, S, norm_g, w_in, fgate_b, q_g, k_g, w_out):
    q, k, v, c = _fox_in(x2, B, S, norm_g, w_in, fgate_b, q_g, k_g, ts=512)
    o = _fox_attn(q, k, v, c, B, S, tq=512)
    return _out_proj1(x2, o.reshape(B * S, FOX_HEADS * FOX_HD), w_out, tm=512)


def _peer_scores_kernel(x_ref, g_ref, wqt_ref, keys_ref, xnt_ref, st_ref):
    xn = _rms(x_ref[...], g_ref[...])
    xnt_ref[...] = xn.T.astype(xnt_ref.dtype)
    qt = _dot_nt(wqt_ref[...], xn.astype(BF16))
    for j in range(2 * PEER_HEADS):
        st_ref[j] = _dot(keys_ref[j], qt[j * PEER_HALF:(j + 1) * PEER_HALF, :].astype(BF16))


def _peer_scores(x2, g, wq, keys, tm):
    T, D = x2.shape
    nq = wq.shape[1]
    wqt = wq.astype(BF16).T
    keys2 = keys.reshape(2 * PEER_HEADS, PEER_NKEYS, PEER_HALF).astype(BF16)
    return pl.pallas_call(
        _peer_scores_kernel,
        grid=(T // tm,),
        in_specs=[pl.BlockSpec((tm, D), lambda i: (i, 0)), pl.BlockSpec((1, D), lambda i: (0, 0)),
                  pl.BlockSpec((nq, D), lambda i: (0, 0)),
                  pl.BlockSpec(keys2.shape, lambda i: (0, 0, 0))],
        out_specs=[pl.BlockSpec((D, tm), lambda i: (0, i)),
                   pl.BlockSpec((2 * PEER_HEADS, PEER_NKEYS, tm), lambda i: (0, 0, i))],
        out_shape=[jax.ShapeDtypeStruct((D, T), BF16),
                   jax.ShapeDtypeStruct((2 * PEER_HEADS, PEER_NKEYS, T), F32)],
        compiler_params=_cparams(("parallel",)),
        name="peer_scores",
    )(x2, g.reshape(1, D), wqt, keys2)


TOPK_SUB = 8
_PEER_CELLS = tuple((r, c) for r in range(PEER_TOPK) for c in range(PEER_TOPK) if (r + 1) * (c + 1) <= PEER_TOPK)


def _top16(s, kio):
    vals, idxs = [], []
    work = s
    for r in range(PEER_TOPK):
        m = jnp.max(work, axis=0)
        idx = jnp.min(jnp.where(work == m[None], kio, PEER_NKEYS), axis=0)
        vals.append(m)
        idxs.append(idx)
        if r + 1 < PEER_TOPK:
            work = jnp.where(kio == idx[None], -jnp.inf, work)
    return vals, idxs


def _peer_topk_kernel(s_ref, na_ref, e1_ref, rk_ref, e2_ref):
    s1 = s_ref[0]
    s2 = s_ref[1]
    kio = lax.broadcasted_iota(jnp.int32, s1.shape, 0)
    v1, i1 = _top16(s1, kio)
    v2, i2 = _top16(s2, kio)
    cand = [v1[r] + v2[c] for r, c in _PEER_CELLS]
    flat = [r * PEER_TOPK + c for r, c in _PEER_CELLS]
    big = PEER_TOPK * PEER_TOPK
    sel_val, sel_row = [], []
    for j in range(PEER_TOPK):
        m = functools.reduce(jnp.maximum, cand)
        idx = functools.reduce(jnp.minimum, [jnp.where(cv == m, f, big) for cv, f in zip(cand, flat)])
        sel_val.append(m)
        sel_row.append(idx >> 4)
        if j + 1 < PEER_TOPK:
            cand = [jnp.where(idx == f, -jnp.inf, cv) for cv, f in zip(cand, flat)]
    z = functools.reduce(jnp.add, [jnp.exp(m - sel_val[0]) for m in sel_val])
    inv_z = 1.0 / z
    n_row = [functools.reduce(jnp.add, [jnp.where(sr == r, 1.0, 0.0) for sr in sel_row]) for r in range(PEER_TOPK)]
    na = jnp.zeros(s1.shape, F32)
    for r in range(PEER_TOPK):
        na = na + jnp.where(kio == i1[r][None], n_row[r][None], 0.0)
    rk = jnp.full(s2.shape, float(PEER_TOPK), F32)
    for c in range(PEER_TOPK):
        rk = rk - jnp.where(kio == i2[c][None], float(PEER_TOPK - c), 0.0)
    na_ref[0] = na
    rk_ref[0] = rk
    e1_ref[0] = jnp.exp(s1 - v1[0][None]) * inv_z[None]
    e2_ref[0] = jnp.exp(s2 - v2[0][None])


def _peer_topk(st4):
    _, nk, nsub, lanes = st4.shape
    out_shape = jax.ShapeDtypeStruct((PEER_HEADS, nk, nsub, lanes), F32)
    out_spec = pl.BlockSpec((1, nk, TOPK_SUB, lanes), lambda i, h: (h, 0, i, 0))
    return pl.pallas_call(
        _peer_topk_kernel,
        grid=(nsub // TOPK_SUB, PEER_HEADS),
        in_specs=[pl.BlockSpec((2, nk, TOPK_SUB, lanes), lambda i, h: (h, 0, i, 0))],
        out_specs=[out_spec] * 4,
        out_shape=[out_shape] * 4,
        compiler_params=_cparams(("parallel", "parallel")),
        name="peer_topk",
    )(st4)


PEER_ABLK = 8


def _gelu(x):
    return 0.5 * x * (1.0 + lax.erf(x * (2.0 ** -0.5)))


def _peer_dense_kernel(xnt_ref, u_ref, vt_ref, na_ref, e1_ref, rk_ref, e2_ref, x_ref, o_ref, acc_ref, m_ref):
    j = pl.program_id(1)

    @pl.when(j == 0)
    def _():
        acc_ref[...] = jnp.zeros_like(acc_ref)

    act = _gelu(_dot(u_ref[...], xnt_ref[...]))
    for al in range(PEER_ABLK):
        rows = slice(al * PEER_NKEYS, (al + 1) * PEER_NKEYS)
        w = None
        for h in range(PEER_HEADS):
            na = na_ref[h, al:al + 1, :]
            e1 = e1_ref[h, al:al + 1, :]
            wh = jnp.where(rk_ref[h] < na, e2_ref[h] * e1, 0.0)
            w = wh if w is None else w + wh
        m_ref[rows, :] = (w * act[rows, :]).astype(m_ref.dtype)
    acc_ref[...] += _dot(vt_ref[...], m_ref[...])

    @pl.when(j == pl.num_programs(1) - 1)
    def _():
        o_ref[...] = x_ref[...] + acc_ref[...].T


def _peer_dense(x2, xnt, u, v, na, e1, rk, e2, tt):
    T, D = x2.shape
    ne = u.shape[0]
    eb = PEER_ABLK * PEER_NKEYS
    ub = u.astype(BF16)
    vt = v.astype(BF16).T
    key_a = pl.BlockSpec((PEER_HEADS, PEER_ABLK, tt), lambda i, j: (0, j, i))
    key_b = pl.BlockSpec((PEER_HEADS, PEER_NKEYS, tt), lambda i, j: (0, 0, i))
    return pl.pallas_call(
        _peer_dense_kernel,
        grid=(T // tt, ne // eb),
        in_specs=[pl.BlockSpec((D, tt), lambda i, j: (0, i)),
                  pl.BlockSpec((eb, D), lambda i, j: (j, 0)),
                  pl.BlockSpec((D, eb), lambda i, j: (0, j)),
                  key_a, key_a, key_b, key_b,
                  pl.BlockSpec((tt, D), lambda i, j: (i, 0))],
        out_specs=pl.BlockSpec((tt, D), lambda i, j: (i, 0)),
        out_shape=jax.ShapeDtypeStruct((T, D), F32),
        scratch_shapes=[pltpu.VMEM((D, tt), F32), pltpu.VMEM((eb, tt), BF16)],
        compiler_params=_cparams(("parallel", "arbitrary")),
        name="peer_dense",
    )(xnt, ub, vt, na, e1, rk, e2, x2)


def _peer(x2, norm_g, wq, keys, u, v):
    T, D = x2.shape
    xnt, st = _peer_scores(x2, norm_g, wq, keys, tm=512)
    st4 = st.reshape(2 * PEER_HEADS, PEER_NKEYS, T // LANES, LANES)
    na, e1, rk, e2 = [a.reshape(PEER_HEADS, PEER_NKEYS, T) for a in _peer_topk(st4)]
    return _peer_dense(x2, xnt, u, v, na, e1, rk, e2, tt=512)


def kernel(x, ev_norm_mix, ev_w_in, ev_conv_w, ev_conv_b, ev_conv_ln_g, ev_conv_ln_b, ev_gate_w2, ev_gate_b,
           ev_gla_norm_g, ev_w_out, od_norm_mix, od_w_in, od_fgate_b, od_q_norm_g, od_k_norm_g, od_w_out,
           ffn_norm, peer_wq, peer_keys, peer_u, peer_v):
    B, S, D = x.shape
    x2 = x.reshape(B * S, D)
    x2 = _even_mix(x2, B, S, ev_norm_mix[0], ev_w_in[0], ev_conv_w[0], ev_conv_b[0], ev_conv_ln_g[0],
                   ev_conv_ln_b[0], ev_gate_w2[0], ev_gate_b[0], ev_gla_norm_g[0], ev_w_out[0])
    x2 = _peer(x2, ffn_norm[0], peer_wq[0], peer_keys[0], peer_u[0], peer_v[0])
    x2 = _fox_mix(x2, B, S, od_norm_mix[0], od_w_in[0], od_fgate_b[0], od_q_norm_g[0], od_k_norm_g[0], od_w_out[0])
    x2 = _peer(x2, ffn_norm[1], peer_wq[1], peer_keys[1], peer_u[1], peer_v[1])
    return x2.reshape(B, S, D)
```

```python
import functools

import jax
import jax.numpy as jnp
from jax import lax
from jax.experimental import pallas as pl
from jax.experimental.pallas import tpu as pltpu

F32 = jnp.float32
BF16 = jnp.bfloat16
HIGHEST = lax.Precision.HIGHEST

EPS = 1e-6
D_MODEL = 1024
CHUNK = 64
CONV_CH = 512
CONV_WIDTH = 31
GLA_HEADS = 4
GLA_DK = 64
GLA_DV = 128
GLA_GATE_RANK = 16
GLA_GATE_TAU = 16.0
FOX_HEADS = 16
FOX_HD = 64
PEER_HEADS = 8
PEER_NKEYS = 128
PEER_TOPK = 16
PEER_HALF = 128

LANES = 128
BF16_ROWS = 16
VMEM_LIMIT = 56 * 1024 * 1024


def _cparams(sem):
    return pltpu.CompilerParams(dimension_semantics=sem, vmem_limit_bytes=VMEM_LIMIT)


def _rms(x, g):
    ms = jnp.mean(x * x, axis=-1, keepdims=True)
    return x * lax.rsqrt(ms + EPS) * g


def _log_sigmoid(z):
    return jnp.minimum(z, 0.0) - jnp.log(1.0 + jnp.exp(-jnp.abs(z)))


def _dot(a, b, **kw):
    return jnp.dot(a, b, preferred_element_type=F32, **kw)


def _dot_nt(a, b, **kw):
    return lax.dot_general(a, b, (((1,), (1,)), ((), ())), preferred_element_type=F32, **kw)


def _even_in_kernel(x_ref, g_ref, w_ref, wkt_ref, wgt_ref,
                    aval_ref, agate_ref, q_ref, v_ref, r_ref, glr_ref, kt_ref, glrt_ref):
    xn = _rms(x_ref[...], g_ref[...]).astype(BF16)
    off = 0
    for o_ref in (aval_ref, agate_ref, q_ref, v_ref, r_ref, glr_ref):
        n = o_ref.shape[-1]
        o_ref[...] = _dot(xn, w_ref[:, off:off + n]).astype(o_ref.dtype)
        off += n
    kt_ref[...] = _dot_nt(wkt_ref[...], xn)
    glrt_ref[...] = _dot_nt(wgt_ref[...], xn)


def _even_in(x2, g, w_in, tm):
    T, D = x2.shape
    hk = GLA_HEADS * GLA_DK
    hv = GLA_HEADS * GLA_DV
    c0, c1, c2, c3, c4, c5 = CONV_CH, 2 * CONV_CH, 2 * CONV_CH + hk, 2 * CONV_CH + 2 * hk, 2 * CONV_CH + 2 * hk + hv, 2 * CONV_CH + 2 * hk + 2 * hv
    wb = w_in.astype(BF16)
    w_tok = jnp.concatenate([wb[:, :c2], wb[:, c3:]], axis=1)
    wkt = wb[:, c2:c3].T
    wgt = wb[:, c5:].T
    widths = (CONV_CH, CONV_CH, hk, hv, hv, GLA_GATE_RANK)
    out_shape = [jax.ShapeDtypeStruct((T, n), F32) for n in widths]
    out_shape += [jax.ShapeDtypeStruct((hk, T), F32), jax.ShapeDtypeStruct((GLA_GATE_RANK, T), F32)]
    out_specs = [pl.BlockSpec((tm, n), lambda i: (i, 0)) for n in widths]
    out_specs += [pl.BlockSpec((hk, tm), lambda i: (0, i)), pl.BlockSpec((GLA_GATE_RANK, tm), lambda i: (0, i))]
    return pl.pallas_call(
        _even_in_kernel,
        grid=(T // tm,),
        in_specs=[
            pl.BlockSpec((tm, D), lambda i: (i, 0)),
            pl.BlockSpec((1, D), lambda i: (0, 0)),
            pl.BlockSpec(w_tok.shape, lambda i: (0, 0)),
            pl.BlockSpec(wkt.shape, lambda i: (0, 0)),
            pl.BlockSpec(wgt.shape, lambda i: (0, 0)),
        ],
        out_specs=out_specs,
        out_shape=out_shape,
        compiler_params=_cparams(("parallel",)),
        name="even_in",
    )(x2, g.reshape(1, D), w_tok, wkt, wgt)


CONV_HALO = 32
CONV_ROWS = 64


def _conv_kernel(val_ref, gate_ref, hval_ref, hgate_ref, w_ref, b_ref, lg_ref, lb_ref, o_ref, ext_ref, *, ts):
    first = pl.program_id(1) == 0
    u_halo = hval_ref[0] * jax.nn.sigmoid(hgate_ref[0])
    ext_ref[0:CONV_HALO, :] = jnp.where(first, 0.0, u_halo)
    ext_ref[CONV_HALO:CONV_HALO + ts, :] = val_ref[0] * jax.nn.sigmoid(gate_ref[0])
    lead = CONV_HALO - (CONV_WIDTH - 1)
    for r0 in range(0, ts, CONV_ROWS):
        acc = jnp.broadcast_to(b_ref[...], (CONV_ROWS, CONV_CH))
        for k in range(CONV_WIDTH):
            acc = acc + w_ref[k:k + 1, :] * ext_ref[r0 + lead + k:r0 + lead + k + CONV_ROWS, :]
        mu = jnp.mean(acc, axis=-1, keepdims=True)
        xc = acc - mu
        y = xc * lax.rsqrt(jnp.mean(xc * xc, axis=-1, keepdims=True) + EPS) * lg_ref[...] + lb_ref[...]
        o_ref[0, r0:r0 + CONV_ROWS, :] = (y * jax.nn.sigmoid(y)).astype(o_ref.dtype)


def _conformer_conv(val, gate, conv_w, conv_b, ln_g, ln_b, ts):
    B, S, C = val.shape
    hb = ts // CONV_HALO
    cur = pl.BlockSpec((1, ts, C), lambda b, i: (b, i, 0))
    halo = pl.BlockSpec((1, CONV_HALO, C), lambda b, i: (b, jnp.maximum(i * hb - 1, 0), 0))
    vec = pl.BlockSpec((1, C), lambda b, i: (0, 0))
    return pl.pallas_call(
        functools.partial(_conv_kernel, ts=ts),
        grid=(B, S // ts),
        in_specs=[cur, cur, halo, halo, pl.BlockSpec((CONV_WIDTH, C), lambda b, i: (0, 0)), vec, vec, vec],
        out_specs=pl.BlockSpec((1, ts, C), lambda b, i: (b, i, 0)),
        out_shape=jax.ShapeDtypeStruct((B, S, C), BF16),
        scratch_shapes=[pltpu.VMEM((CONV_HALO + ts, C), F32)],
        compiler_params=_cparams(("parallel", "parallel")),
        name="conformer_conv",
    )(val, gate, val, gate, conv_w, conv_b.reshape(1, C), ln_g.reshape(1, C), ln_b.reshape(1, C))


def _gla_kernel(q_ref, v_ref, r_ref, glr_ref, kt_ref, glrt_ref, w2_ref, w2t_ref, gb_ref, gbt_ref, ng_ref,
                o_ref, state_ref, *, ts):
    @pl.when(pl.program_id(1) == 0)
    def _():
        state_ref[...] = jnp.zeros_like(state_ref)

    L = CHUNK
    row = lax.broadcasted_iota(jnp.int32, (L, L), 0)
    col = lax.broadcasted_iota(jnp.int32, (L, L), 1)
    causal = row >= col
    tril = causal.astype(F32)
    triu = (row <= col).astype(F32)
    mid = L // 2 - 1
    scale = GLA_DK ** -0.5

    la = _log_sigmoid(_dot(glr_ref[0], w2_ref[...], precision=HIGHEST) + gb_ref[...]) * (1.0 / GLA_GATE_TAU)
    lat = _log_sigmoid(_dot(w2t_ref[...], glrt_ref[...], precision=HIGHEST) + gbt_ref[...]) * (1.0 / GLA_GATE_TAU)

    for c in range(ts // L):
        sl = slice(c * L, (c + 1) * L)
        b = _dot(tril, la[sl, :], precision=HIGHEST)
        bt = _dot(lat[:, sl], triu, precision=HIGHEST)
        qc = q_ref[0, sl, :] * scale
        q_in = (qc * jnp.exp(b)).astype(BF16)
        q_a = (qc * jnp.exp(b - b[mid:mid + 1, :])).astype(BF16)
        kt = kt_ref[:, sl]
        bt_last = bt[:, L - 1:L]
        k_a = (kt * jnp.exp(bt[:, mid:mid + 1] - bt)).astype(BF16)
        k_dec = (kt * jnp.exp(bt_last - bt)).astype(BF16)
        v_c = v_ref[0, sl, :].astype(BF16)
        r_c = r_ref[0, sl, :]
        for h in range(GLA_HEADS):
            ks = slice(h * GLA_DK, (h + 1) * GLA_DK)
            vs = slice(h * GLA_DV, (h + 1) * GLA_DV)
            state = state_ref[h]
            attn = jnp.where(causal, _dot(q_a[:, ks], k_a[ks, :]), 0.0)
            o = _dot(q_in[:, ks], state.astype(BF16)) + _dot(attn.astype(BF16), v_c[:, vs])
            state_ref[h] = state * jnp.exp(bt_last[ks, :]) + _dot(k_dec[ks, :], v_c[:, vs])
            y = _rms(o, ng_ref[...])
            rg = r_c[:, vs]
            o_ref[0, sl, vs] = (y * (rg * jax.nn.sigmoid(rg))).astype(o_ref.dtype)


def _gla(q, v, r, glr, kt, glrt, gate_w2, gate_b, norm_g, B, S, ts):
    hk = GLA_HEADS * GLA_DK
    hv = GLA_HEADS * GLA_DV
    nt = S // ts
    tok = lambda n: pl.BlockSpec((1, ts, n), lambda b, i: (b, i, 0))
    tr = lambda n: pl.BlockSpec((n, ts), lambda b, i: (0, b * nt + i))
    full = lambda shp: pl.BlockSpec(shp, lambda b, i: (0,) * len(shp))
    return pl.pallas_call(
        functools.partial(_gla_kernel, ts=ts),
        grid=(B, nt),
        in_specs=[tok(hk), tok(hv), tok(hv), tok(GLA_GATE_RANK), tr(hk), tr(GLA_GATE_RANK),
                  full((GLA_GATE_RANK, hk)), full((hk, GLA_GATE_RANK)), full((1, hk)), full((hk, 1)), full((1, GLA_DV))],
        out_specs=tok(hv),
        out_shape=jax.ShapeDtypeStruct((B, S, hv), BF16),
        scratch_shapes=[pltpu.VMEM((GLA_HEADS, GLA_DK, GLA_DV), F32)],
        compiler_params=_cparams(("parallel", "arbitrary")),
        name="gla",
    )(q.reshape(B, S, hk), v.reshape(B, S, hv), r.reshape(B, S, hv), glr.reshape(B, S, GLA_GATE_RANK), kt, glrt,
      gate_w2, gate_w2.T, gate_b.reshape(1, hk), gate_b.reshape(hk, 1), norm_g.reshape(1, GLA_DV))


def _out2_kernel(x_ref, a_ref, b_ref, wa_ref, wb_ref, o_ref):
    o_ref[...] = x_ref[...] + _dot(a_ref[...], wa_ref[...]) + _dot(b_ref[...], wb_ref[...])


def _out_proj2(x2, a, b, w_out, tm):
    T, D = x2.shape
    ka, kb = a.shape[1], b.shape[1]
    wb16 = w_out.astype(BF16)
    return pl.pallas_call(
        _out2_kernel,
        grid=(T // tm,),
        in_specs=[pl.BlockSpec((tm, D), lambda i: (i, 0)), pl.BlockSpec((tm, ka), lambda i: (i, 0)),
                  pl.BlockSpec((tm, kb), lambda i: (i, 0)), pl.BlockSpec((ka, D), lambda i: (0, 0)),
                  pl.BlockSpec((kb, D), lambda i: (0, 0))],
        out_specs=pl.BlockSpec((tm, D), lambda i: (i, 0)),
        out_shape=jax.ShapeDtypeStruct((T, D), F32),
        compiler_params=_cparams(("parallel",)),
        name="out_proj2",
    )(x2, a, b, wb16[:ka], wb16[ka:])


def _out1_kernel(x_ref, a_ref, w_ref, o_ref):
    o_ref[...] = x_ref[...] + _dot(a_ref[...], w_ref[...])


def _out_proj1(x2, a, w_out, tm):
    T, D = x2.shape
    ka = a.shape[1]
    return pl.pallas_call(
        _out1_kernel,
        grid=(T // tm,),
        in_specs=[pl.BlockSpec((tm, D), lambda i: (i, 0)), pl.BlockSpec((tm, ka), lambda i: (i, 0)),
                  pl.BlockSpec((ka, D), lambda i: (0, 0))],
        out_specs=pl.BlockSpec((tm, D), lambda i: (i, 0)),
        out_shape=jax.ShapeDtypeStruct((T, D), F32),
        compiler_params=_cparams(("parallel",)),
        name="out_proj1",
    )(x2, a, w_out.astype(BF16))


def _even_mix(x2, B, S, norm_g, w_in, conv_w, conv_b, ln_g, ln_b, gate_w2, gate_b, gla_norm_g, w_out):
    a_val, a_gate, q, v, r, glr, kt, glrt = _even_in(x2, norm_g, w_in, tm=512)
    y_a = _conformer_conv(a_val.reshape(B, S, CONV_CH), a_gate.reshape(B, S, CONV_CH), conv_w, conv_b, ln_g, ln_b, ts=512)
    y_b = _gla(q, v, r, glr, kt, glrt, gate_w2, gate_b, gla_norm_g, B, S, ts=256)
    return _out_proj2(x2, y_a.reshape(B * S, CONV_CH), y_b.reshape(B * S, GLA_HEADS * GLA_DV), w_out, tm=512)


def _fox_in_kernel(x_ref, g_ref, w_ref, wft_ref, fb_ref, qg_ref, kg_ref, q_ref, k_ref, v_ref, c_ref, carry_ref, *, ts):
    @pl.when(pl.program_id(1) == 0)
    def _():
        carry_ref[...] = jnp.zeros_like(carry_ref)

    xn = _rms(x_ref[0], g_ref[...]).astype(BF16)
    hd = FOX_HEADS * FOX_HD
    q = _dot(xn, w_ref[:, 0:hd])
    k = _dot(xn, w_ref[:, hd:2 * hd])
    v = _dot(xn, w_ref[:, 2 * hd:3 * hd])
    scale = FOX_HD ** -0.5
    for h in range(FOX_HEADS):
        hs = slice(h * FOX_HD, (h + 1) * FOX_HD)
        q_ref[0, h] = (_rms(q[:, hs], qg_ref[...]) * scale).astype(q_ref.dtype)
        k_ref[0, h] = _rms(k[:, hs], kg_ref[...]).astype(k_ref.dtype)
        v_ref[0, h] = v[:, hs].astype(v_ref.dtype)
    lf = _log_sigmoid(_dot_nt(wft_ref[...], xn) + fb_ref[...])
    row = lax.broadcasted_iota(jnp.int32, (ts, ts), 0)
    col = lax.broadcasted_iota(jnp.int32, (ts, ts), 1)
    c = _dot(lf, (row <= col).astype(F32), precision=HIGHEST) + carry_ref[...]
    c_ref[0] = c
    carry_ref[...] = c[:, ts - 1:ts]


def _fox_in(x2, B, S, g, w_in, fgate_b, q_g, k_g, ts):
    D = x2.shape[1]
    hd = FOX_HEADS * FOX_HD
    wb = w_in.astype(BF16)
    w_tok = wb[:, :3 * hd]
    wft = wb[:, 3 * hd:].T
    head_shape = jax.ShapeDtypeStruct((B, FOX_HEADS, S, FOX_HD), BF16)
    head_spec = pl.BlockSpec((1, FOX_HEADS, ts, FOX_HD), lambda b, i: (b, 0, i, 0))
    full = lambda shp: pl.BlockSpec(shp, lambda b, i: (0,) * len(shp))
    return pl.pallas_call(
        functools.partial(_fox_in_kernel, ts=ts),
        grid=(B, S // ts),
        in_specs=[pl.BlockSpec((1, ts, D), lambda b, i: (b, i, 0)), full((1, D)), full(w_tok.shape), full(wft.shape),
                  full((FOX_HEADS, 1)), full((1, FOX_HD)), full((1, FOX_HD))],
        out_specs=[head_spec, head_spec, head_spec, pl.BlockSpec((1, FOX_HEADS, ts), lambda b, i: (b, 0, i))],
        out_shape=[head_shape, head_shape, head_shape, jax.ShapeDtypeStruct((B, FOX_HEADS, S), F32)],
        scratch_shapes=[pltpu.VMEM((FOX_HEADS, 1), F32)],
        compiler_params=_cparams(("parallel", "arbitrary")),
        name="fox_in",
    )(x2.reshape(B, S, D), g.reshape(1, D), w_tok, wft, fgate_b.reshape(FOX_HEADS, 1), q_g.reshape(1, FOX_HD),
      k_g.reshape(1, FOX_HD))


FOX_PAIR = 2


FOX_EXP_ZERO = 110.0


def _fox_attn_kernel(bound_ref, q_ref, k_ref, v_ref, c_ref, o_ref, *, tq):
    qi = pl.program_id(2)
    row = lax.broadcasted_iota(jnp.int32, (tq, tq), 0)
    col = lax.broadcasted_iota(jnp.int32, (tq, tq), 1)
    causal = row >= col
    nt = c_ref.shape[3]
    tile_id = lax.broadcasted_iota(jnp.int32, (nt, 1), 0)
    outs = []
    for hh in range(FOX_PAIR):
        q = q_ref[0, hh]
        c_q = jnp.sum(jnp.where(tile_id == qi, c_ref[0, 0, hh, :, 0:1], 0.0), axis=0, keepdims=True)
        c_k = c_ref[0, 0, hh, :, tq - 1:tq]
        dead = (tile_id < qi) & (2.0 * bound_ref[0, 0] + c_q - c_k < -FOX_EXP_ZERO)
        lo = jnp.sum(dead.astype(jnp.int32))

        def step(ki, carry, masked):
            m, l, acc = carry
            k0 = pl.multiple_of(ki * tq, tq)
            k = k_ref[0, hh, pl.ds(k0, tq), :]
            v = v_ref[0, hh, pl.ds(k0, tq), :]
            s = _dot_nt(q, k) - c_ref[0, 0, hh, pl.ds(ki, 1), :]
            if masked:
                s = jnp.where(causal, s, -jnp.inf)
            m_new = jnp.maximum(m, jnp.max(s, axis=-1, keepdims=True))
            p = jnp.exp(s - m_new)
            alpha = jnp.exp(m - m_new)
            l = alpha * l + jnp.sum(p, axis=-1, keepdims=True)
            acc = alpha * acc + _dot(p.astype(BF16), v)
            return m_new, l, acc

        init = (jnp.full((tq, 1), -jnp.inf, F32), jnp.zeros((tq, 1), F32), jnp.zeros((tq, FOX_HD), F32))
        carry = lax.fori_loop(lo, qi, functools.partial(step, masked=False), init)
        _, l, acc = step(qi, carry, True)
        outs.append(acc / l)
    o_ref[0] = jnp.concatenate(outs, axis=-1).astype(o_ref.dtype)


def _fox_attn(q, k, v, c, bound, B, S, tq):
    npair = FOX_HEADS // FOX_PAIR
    c5 = c.reshape(B, npair, FOX_PAIR, S // tq, tq)
    qspec = pl.BlockSpec((1, FOX_PAIR, tq, FOX_HD), lambda b, h, i: (b, h, i, 0))
    kvspec = pl.BlockSpec((1, FOX_PAIR, S, FOX_HD), lambda b, h, i: (b, h, 0, 0))
    return pl.pallas_call(
        functools.partial(_fox_attn_kernel, tq=tq),
        grid=(B, npair, S // tq),
        in_specs=[pl.BlockSpec(memory_space=pltpu.SMEM), qspec, kvspec, kvspec,
                  pl.BlockSpec((1, 1, FOX_PAIR, S // tq, tq), lambda b, h, i: (b, h, 0, 0, 0))],
        out_specs=pl.BlockSpec((1, tq, FOX_PAIR * FOX_HD), lambda b, h, i: (b, i, h)),
        out_shape=jax.ShapeDtypeStruct((B, S, FOX_HEADS * FOX_HD), BF16),
        compiler_params=_cparams(("parallel", "parallel", "arbitrary")),
        name="fox_attn",
    )(bound, q, k, v, c5)


FOX_NORM_SLACK = 1.025


def _fox_mix(x2, B, S, norm_g, w_in, fgate_b, q_g, k_g, w_out):
    q, k, v, c = _fox_in(x2, B, S, norm_g, w_in, fgate_b, q_g, k_g, ts=512)
    bound = (FOX_NORM_SLACK * FOX_HD ** 0.5 * jnp.max(jnp.abs(q_g)) * jnp.max(jnp.abs(k_g))).reshape(1, 1).astype(F32)
    o = _fox_attn(q, k, v, c, bound, B, S, tq=512)
    return _out_proj1(x2, o.reshape(B * S, FOX_HEADS * FOX_HD), w_out, tm=512)


def _peer_scores_kernel(x_ref, g_ref, wqt_ref, keys_ref, xnt_ref, st_ref):
    xn = _rms(x_ref[...], g_ref[...])
    xnt_ref[...] = xn.T.astype(xnt_ref.dtype)
    qt = _dot_nt(wqt_ref[...], xn.astype(BF16))
    for j in range(2 * PEER_HEADS):
        st_ref[j] = _dot(keys_ref[j], qt[j * PEER_HALF:(j + 1) * PEER_HALF, :].astype(BF16))


def _peer_scores(x2, g, wq, keys, tm):
    T, D = x2.shape
    nq = wq.shape[1]
    wqt = wq.astype(BF16).T
    keys2 = keys.reshape(2 * PEER_HEADS, PEER_NKEYS, PEER_HALF).astype(BF16)
    return pl.pallas_call(
        _peer_scores_kernel,
        grid=(T // tm,),
        in_specs=[pl.BlockSpec((tm, D), lambda i: (i, 0)), pl.BlockSpec((1, D), lambda i: (0, 0)),
                  pl.BlockSpec((nq, D), lambda i: (0, 0)),
                  pl.BlockSpec(keys2.shape, lambda i: (0, 0, 0))],
        out_specs=[pl.BlockSpec((D, tm), lambda i: (0, i)),
                   pl.BlockSpec((2 * PEER_HEADS, PEER_NKEYS, tm), lambda i: (0, 0, i))],
        out_shape=[jax.ShapeDtypeStruct((D, T), BF16),
                   jax.ShapeDtypeStruct((2 * PEER_HEADS, PEER_NKEYS, T), F32)],
        compiler_params=_cparams(("parallel",)),
        name="peer_scores",
    )(x2, g.reshape(1, D), wqt, keys2)


TOPK_SUB = 8
_PEER_CELLS = tuple((r, c) for r in range(PEER_TOPK) for c in range(PEER_TOPK) if (r + 1) * (c + 1) <= PEER_TOPK)


def _top16(s, kio):
    vals, idxs = [], []
    work = s
    for r in range(PEER_TOPK):
        m = jnp.max(work, axis=0)
        idx = jnp.min(jnp.where(work == m[None], kio, PEER_NKEYS), axis=0)
        vals.append(m)
        idxs.append(idx)
        if r + 1 < PEER_TOPK:
            work = jnp.where(kio == idx[None], -jnp.inf, work)
    return vals, idxs


def _peer_topk_kernel(s_ref, na_ref, e1_ref, rk_ref, e2_ref):
    s1 = s_ref[0]
    s2 = s_ref[1]
    kio = lax.broadcasted_iota(jnp.int32, s1.shape, 0)
    v1, i1 = _top16(s1, kio)
    v2, i2 = _top16(s2, kio)
    cand = [v1[r] + v2[c] for r, c in _PEER_CELLS]
    flat = [r * PEER_TOPK + c for r, c in _PEER_CELLS]
    big = PEER_TOPK * PEER_TOPK
    sel_val, sel_row = [], []
    for j in range(PEER_TOPK):
        m = functools.reduce(jnp.maximum, cand)
        idx = functools.reduce(jnp.minimum, [jnp.where(cv == m, f, big) for cv, f in zip(cand, flat)])
        sel_val.append(m)
        sel_row.append(idx >> 4)
        if j + 1 < PEER_TOPK:
            cand = [jnp.where(idx == f, -jnp.inf, cv) for cv, f in zip(cand, flat)]
    z = functools.reduce(jnp.add, [jnp.exp(m - sel_val[0]) for m in sel_val])
    inv_z = 1.0 / z
    n_row = [functools.reduce(jnp.add, [jnp.where(sr == r, 1.0, 0.0) for sr in sel_row]) for r in range(PEER_TOPK)]
    na = jnp.zeros(s1.shape, F32)
    for r in range(PEER_TOPK):
        na = na + jnp.where(kio == i1[r][None], n_row[r][None], 0.0)
    rk = jnp.full(s2.shape, float(PEER_TOPK), F32)
    for c in range(PEER_TOPK):
        rk = rk - jnp.where(kio == i2[c][None], float(PEER_TOPK - c), 0.0)
    na_ref[0] = na
    rk_ref[0] = rk
    e1_ref[0] = jnp.exp(s1 - v1[0][None]) * inv_z[None]
    e2_ref[0] = jnp.exp(s2 - v2[0][None])


def _peer_topk(st4):
    _, nk, nsub, lanes = st4.shape
    out_shape = jax.ShapeDtypeStruct((PEER_HEADS, nk, nsub, lanes), F32)
    out_spec = pl.BlockSpec((1, nk, TOPK_SUB, lanes), lambda i, h: (h, 0, i, 0))
    return pl.pallas_call(
        _peer_topk_kernel,
        grid=(nsub // TOPK_SUB, PEER_HEADS),
        in_specs=[pl.BlockSpec((2, nk, TOPK_SUB, lanes), lambda i, h: (h, 0, i, 0))],
        out_specs=[out_spec] * 4,
        out_shape=[out_shape] * 4,
        compiler_params=_cparams(("parallel", "parallel")),
        name="peer_topk",
    )(st4)


PEER_ABLK = 8


def _gelu(x):
    return 0.5 * x * (1.0 + lax.erf(x * (2.0 ** -0.5)))


def _peer_dense_kernel(xnt_ref, u_ref, vt_ref, na_ref, e1_ref, rk_ref, e2_ref, x_ref, o_ref, acc_ref, m_ref):
    j = pl.program_id(1)

    @pl.when(j == 0)
    def _():
        acc_ref[...] = jnp.zeros_like(acc_ref)

    tt = m_ref.shape[1]
    zero = jnp.zeros((BF16_ROWS, tt), BF16)
    for al in range(PEER_ABLK):
        rows_a = []
        for h in range(PEER_HEADS):
            na = jnp.broadcast_to(na_ref[h, al:al + 1, :], (BF16_ROWS, tt)).astype(BF16)
            e1 = jnp.broadcast_to(e1_ref[h, al:al + 1, :], (BF16_ROWS, tt)).astype(BF16)
            rows_a.append((na, e1))
        for g in range(PEER_NKEYS // BF16_ROWS):
            bs = slice(g * BF16_ROWS, (g + 1) * BF16_ROWS)
            w = None
            for h in range(PEER_HEADS):
                na, e1 = rows_a[h]
                wh = jnp.where(rk_ref[h, bs, :] < na, e2_ref[h, bs, :] * e1, zero)
                w = wh if w is None else w + wh
            m_ref[al * PEER_NKEYS + g * BF16_ROWS:al * PEER_NKEYS + (g + 1) * BF16_ROWS, :] = w
    act = _gelu(_dot(u_ref[...], xnt_ref[...]))
    m_ref[...] = m_ref[...] * act.astype(BF16)
    acc_ref[...] += _dot(vt_ref[...], m_ref[...])

    @pl.when(j == pl.num_programs(1) - 1)
    def _():
        o_ref[...] = x_ref[...] + acc_ref[...].T


def _peer_dense(x2, xnt, u, v, na, e1, rk, e2, tt):
    T, D = x2.shape
    ne = u.shape[0]
    eb = PEER_ABLK * PEER_NKEYS
    ub = u.astype(BF16)
    vt = v.astype(BF16).T
    key_a = pl.BlockSpec((PEER_HEADS, PEER_ABLK, tt), lambda i, j: (0, j, i))
    key_b = pl.BlockSpec((PEER_HEADS, PEER_NKEYS, tt), lambda i, j: (0, 0, i))
    return pl.pallas_call(
        _peer_dense_kernel,
        grid=(T // tt, ne // eb),
        in_specs=[pl.BlockSpec((D, tt), lambda i, j: (0, i)),
                  pl.BlockSpec((eb, D), lambda i, j: (j, 0)),
                  pl.BlockSpec((D, eb), lambda i, j: (0, j)),
                  key_a, key_a, key_b, key_b,
                  pl.BlockSpec((tt, D), lambda i, j: (i, 0))],
        out_specs=pl.BlockSpec((tt, D), lambda i, j: (i, 0)),
        out_shape=jax.ShapeDtypeStruct((T, D), F32),
        scratch_shapes=[pltpu.VMEM((D, tt), F32), pltpu.VMEM((eb, tt), BF16)],
        compiler_params=_cparams(("parallel", "arbitrary")),
        name="peer_dense",
    )(xnt, ub, vt, na, e1, rk, e2, x2)


def _peer(x2, norm_g, wq, keys, u, v):
    T, D = x2.shape
    xnt, st = _peer_scores(x2, norm_g, wq, keys, tm=512)
    st4 = st.reshape(2 * PEER_HEADS, PEER_NKEYS, T // LANES, LANES)
    na, e1, rk, e2 = [a.reshape(PEER_HEADS, PEER_NKEYS, T) for a in _peer_topk(st4)]
    return _peer_dense(x2, xnt, u, v, na, e1, rk.astype(BF16), e2.astype(BF16), tt=512)


def kernel(x, ev_norm_mix, ev_w_in, ev_conv_w, ev_conv_b, ev_conv_ln_g, ev_conv_ln_b, ev_gate_w2, ev_gate_b,
           ev_gla_norm_g, ev_w_out, od_norm_mix, od_w_in, od_fgate_b, od_q_norm_g, od_k_norm_g, od_w_out,
           ffn_norm, peer_wq, peer_keys, peer_u, peer_v):
    B, S, D = x.shape
    x2 = x.reshape(B * S, D)
    x2 = _even_mix(x2, B, S, ev_norm_mix[0], ev_w_in[0], ev_conv_w[0], ev_conv_b[0], ev_conv_ln_g[0],
                   ev_conv_ln_b[0], ev_gate_w2[0], ev_gate_b[0], ev_gla_norm_g[0], ev_w_out[0])
    x2 = _peer(x2, ffn_norm[0], peer_wq[0], peer_keys[0], peer_u[0], peer_v[0])
    x2 = _fox_mix(x2, B, S, od_norm_mix[0], od_w_in[0], od_fgate_b[0], od_q_norm_g[0], od_k_norm_g[0], od_w_out[0])
    x2 = _peer(x2, ffn_norm[1], peer_wq[1], peer_keys[1], peer_u[1], peer_v[1])
    return x2.reshape(B, S, D)
```

```python
import functools

import jax
import jax.numpy as jnp
from jax import lax
from jax.experimental import pallas as pl
from jax.experimental.pallas import tpu as pltpu

F32 = jnp.float32
BF16 = jnp.bfloat16
HIGHEST = lax.Precision.HIGHEST

EPS = 1e-6
D_MODEL = 1024
CHUNK = 64
CONV_CH = 512
CONV_WIDTH = 31
GLA_HEADS = 4
GLA_DK = 64
GLA_DV = 128
GLA_GATE_RANK = 16
GLA_GATE_TAU = 16.0
FOX_HEADS = 16
FOX_HD = 64
PEER_HEADS = 8
PEER_NKEYS = 128
PEER_TOPK = 16
PEER_HALF = 128

LANES = 128
BF16_ROWS = 16
VMEM_LIMIT = 56 * 1024 * 1024


def _cparams(sem):
    return pltpu.CompilerParams(dimension_semantics=sem, vmem_limit_bytes=VMEM_LIMIT)


def _rms(x, g):
    ms = jnp.mean(x * x, axis=-1, keepdims=True)
    return x * lax.rsqrt(ms + EPS) * g


def _log_sigmoid(z):
    return jnp.minimum(z, 0.0) - jnp.log(1.0 + jnp.exp(-jnp.abs(z)))


def _dot(a, b, **kw):
    return jnp.dot(a, b, preferred_element_type=F32, **kw)


def _dot_nt(a, b, **kw):
    return lax.dot_general(a, b, (((1,), (1,)), ((), ())), preferred_element_type=F32, **kw)


def _even_in_kernel(x_ref, g_ref, w_ref, wkt_ref, wgt_ref,
                    aval_ref, agate_ref, q_ref, v_ref, r_ref, glr_ref, kt_ref, glrt_ref):
    xn = _rms(x_ref[...], g_ref[...]).astype(BF16)
    off = 0
    for o_ref in (aval_ref, agate_ref, q_ref, v_ref, r_ref, glr_ref):
        n = o_ref.shape[-1]
        o_ref[...] = _dot(xn, w_ref[:, off:off + n]).astype(o_ref.dtype)
        off += n
    kt_ref[...] = _dot_nt(wkt_ref[...], xn)
    glrt_ref[...] = _dot_nt(wgt_ref[...], xn)


def _even_in(x2, g, w_in, tm):
    T, D = x2.shape
    hk = GLA_HEADS * GLA_DK
    hv = GLA_HEADS * GLA_DV
    c0, c1, c2, c3, c4, c5 = CONV_CH, 2 * CONV_CH, 2 * CONV_CH + hk, 2 * CONV_CH + 2 * hk, 2 * CONV_CH + 2 * hk + hv, 2 * CONV_CH + 2 * hk + 2 * hv
    wb = w_in.astype(BF16)
    w_tok = jnp.concatenate([wb[:, :c2], wb[:, c3:]], axis=1)
    wkt = wb[:, c2:c3].T
    wgt = wb[:, c5:].T
    widths = (CONV_CH, CONV_CH, hk, hv, hv, GLA_GATE_RANK)
    out_shape = [jax.ShapeDtypeStruct((T, n), F32) for n in widths]
    out_shape += [jax.ShapeDtypeStruct((hk, T), F32), jax.ShapeDtypeStruct((GLA_GATE_RANK, T), F32)]
    out_specs = [pl.BlockSpec((tm, n), lambda i: (i, 0)) for n in widths]
    out_specs += [pl.BlockSpec((hk, tm), lambda i: (0, i)), pl.BlockSpec((GLA_GATE_RANK, tm), lambda i: (0, i))]
    return pl.pallas_call(
        _even_in_kernel,
        grid=(T // tm,),
        in_specs=[
            pl.BlockSpec((tm, D), lambda i: (i, 0)),
            pl.BlockSpec((1, D), lambda i: (0, 0)),
            pl.BlockSpec(w_tok.shape, lambda i: (0, 0)),
            pl.BlockSpec(wkt.shape, lambda i: (0, 0)),
            pl.BlockSpec(wgt.shape, lambda i: (0, 0)),
        ],
        out_specs=out_specs,
        out_shape=out_shape,
        compiler_params=_cparams(("parallel",)),
        name="even_in",
    )(x2, g.reshape(1, D), w_tok, wkt, wgt)


CONV_HALO = 32
CONV_ROWS = 64


def _conv_kernel(val_ref, gate_ref, hval_ref, hgate_ref, w_ref, b_ref, lg_ref, lb_ref, o_ref, ext_ref, *, ts):
    first = pl.program_id(1) == 0
    u_halo = hval_ref[0] * jax.nn.sigmoid(hgate_ref[0])
    ext_ref[0:CONV_HALO, :] = jnp.where(first, 0.0, u_halo)
    ext_ref[CONV_HALO:CONV_HALO + ts, :] = val_ref[0] * jax.nn.sigmoid(gate_ref[0])
    lead = CONV_HALO - (CONV_WIDTH - 1)
    for r0 in range(0, ts, CONV_ROWS):
        acc = jnp.broadcast_to(b_ref[...], (CONV_ROWS, CONV_CH))
        for k in range(CONV_WIDTH):
            acc = acc + w_ref[k:k + 1, :] * ext_ref[r0 + lead + k:r0 + lead + k + CONV_ROWS, :]
        mu = jnp.mean(acc, axis=-1, keepdims=True)
        xc = acc - mu
        y = xc * lax.rsqrt(jnp.mean(xc * xc, axis=-1, keepdims=True) + EPS) * lg_ref[...] + lb_ref[...]
        o_ref[0, r0:r0 + CONV_ROWS, :] = (y * jax.nn.sigmoid(y)).astype(o_ref.dtype)


def _conformer_conv(val, gate, conv_w, conv_b, ln_g, ln_b, ts):
    B, S, C = val.shape
    hb = ts // CONV_HALO
    cur = pl.BlockSpec((1, ts, C), lambda b, i: (b, i, 0))
    halo = pl.BlockSpec((1, CONV_HALO, C), lambda b, i: (b, jnp.maximum(i * hb - 1, 0), 0))
    vec = pl.BlockSpec((1, C), lambda b, i: (0, 0))
    return pl.pallas_call(
        functools.partial(_conv_kernel, ts=ts),
        grid=(B, S // ts),
        in_specs=[cur, cur, halo, halo, pl.BlockSpec((CONV_WIDTH, C), lambda b, i: (0, 0)), vec, vec, vec],
        out_specs=pl.BlockSpec((1, ts, C), lambda b, i: (b, i, 0)),
        out_shape=jax.ShapeDtypeStruct((B, S, C), BF16),
        scratch_shapes=[pltpu.VMEM((CONV_HALO + ts, C), F32)],
        compiler_params=_cparams(("parallel", "parallel")),
        name="conformer_conv",
    )(val, gate, val, gate, conv_w, conv_b.reshape(1, C), ln_g.reshape(1, C), ln_b.reshape(1, C))


def _gla_kernel(q_ref, v_ref, r_ref, glr_ref, kt_ref, glrt_ref, w2_ref, w2t_ref, gb_ref, gbt_ref, ng_ref,
                o_ref, state_ref, *, ts):
    @pl.when(pl.program_id(1) == 0)
    def _():
        state_ref[...] = jnp.zeros_like(state_ref)

    L = CHUNK
    row = lax.broadcasted_iota(jnp.int32, (L, L), 0)
    col = lax.broadcasted_iota(jnp.int32, (L, L), 1)
    causal = row >= col
    tril = causal.astype(F32)
    triu = (row <= col).astype(F32)
    mid = L // 2 - 1
    scale = GLA_DK ** -0.5

    la = _log_sigmoid(_dot(glr_ref[0], w2_ref[...], precision=HIGHEST) + gb_ref[...]) * (1.0 / GLA_GATE_TAU)
    lat = _log_sigmoid(_dot(w2t_ref[...], glrt_ref[...], precision=HIGHEST) + gbt_ref[...]) * (1.0 / GLA_GATE_TAU)

    for c in range(ts // L):
        sl = slice(c * L, (c + 1) * L)
        b = _dot(tril, la[sl, :], precision=HIGHEST)
        bt = _dot(lat[:, sl], triu, precision=HIGHEST)
        qc = q_ref[0, sl, :] * scale
        q_in = (qc * jnp.exp(b)).astype(BF16)
        q_a = (qc * jnp.exp(b - b[mid:mid + 1, :])).astype(BF16)
        kt = kt_ref[:, sl]
        bt_last = bt[:, L - 1:L]
        k_a = (kt * jnp.exp(bt[:, mid:mid + 1] - bt)).astype(BF16)
        k_dec = (kt * jnp.exp(bt_last - bt)).astype(BF16)
        v_c = v_ref[0, sl, :].astype(BF16)
        r_c = r_ref[0, sl, :]
        for h in range(GLA_HEADS):
            ks = slice(h * GLA_DK, (h + 1) * GLA_DK)
            vs = slice(h * GLA_DV, (h + 1) * GLA_DV)
            state = state_ref[h]
            attn = jnp.where(causal, _dot(q_a[:, ks], k_a[ks, :]), 0.0)
            o = _dot(q_in[:, ks], state.astype(BF16)) + _dot(attn.astype(BF16), v_c[:, vs])
            state_ref[h] = state * jnp.exp(bt_last[ks, :]) + _dot(k_dec[ks, :], v_c[:, vs])
            y = _rms(o, ng_ref[...])
            rg = r_c[:, vs]
            o_ref[0, sl, vs] = (y * (rg * jax.nn.sigmoid(rg))).astype(o_ref.dtype)


def _gla(q, v, r, glr, kt, glrt, gate_w2, gate_b, norm_g, B, S, ts):
    hk = GLA_HEADS * GLA_DK
    hv = GLA_HEADS * GLA_DV
    nt = S // ts
    tok = lambda n: pl.BlockSpec((1, ts, n), lambda b, i: (b, i, 0))
    tr = lambda n: pl.BlockSpec((n, ts), lambda b, i: (0, b * nt + i))
    full = lambda shp: pl.BlockSpec(shp, lambda b, i: (0,) * len(shp))
    return pl.pallas_call(
        functools.partial(_gla_kernel, ts=ts),
        grid=(B, nt),
        in_specs=[tok(hk), tok(hv), tok(hv), tok(GLA_GATE_RANK), tr(hk), tr(GLA_GATE_RANK),
                  full((GLA_GATE_RANK, hk)), full((hk, GLA_GATE_RANK)), full((1, hk)), full((hk, 1)), full((1, GLA_DV))],
        out_specs=tok(hv),
        out_shape=jax.ShapeDtypeStruct((B, S, hv), BF16),
        scratch_shapes=[pltpu.VMEM((GLA_HEADS, GLA_DK, GLA_DV), F32)],
        compiler_params=_cparams(("parallel", "arbitrary")),
        name="gla",
    )(q.reshape(B, S, hk), v.reshape(B, S, hv), r.reshape(B, S, hv), glr.reshape(B, S, GLA_GATE_RANK), kt, glrt,
      gate_w2, gate_w2.T, gate_b.reshape(1, hk), gate_b.reshape(hk, 1), norm_g.reshape(1, GLA_DV))


def _out2_kernel(x_ref, a_ref, b_ref, wa_ref, wb_ref, o_ref):
    o_ref[...] = x_ref[...] + _dot(a_ref[...], wa_ref[...]) + _dot(b_ref[...], wb_ref[...])


def _out_proj2(x2, a, b, w_out, tm):
    T, D = x2.shape
    ka, kb = a.shape[1], b.shape[1]
    wb16 = w_out.astype(BF16)
    return pl.pallas_call(
        _out2_kernel,
        grid=(T // tm,),
        in_specs=[pl.BlockSpec((tm, D), lambda i: (i, 0)), pl.BlockSpec((tm, ka), lambda i: (i, 0)),
                  pl.BlockSpec((tm, kb), lambda i: (i, 0)), pl.BlockSpec((ka, D), lambda i: (0, 0)),
                  pl.BlockSpec((kb, D), lambda i: (0, 0))],
        out_specs=pl.BlockSpec((tm, D), lambda i: (i, 0)),
        out_shape=jax.ShapeDtypeStruct((T, D), F32),
        compiler_params=_cparams(("parallel",)),
        name="out_proj2",
    )(x2, a, b, wb16[:ka], wb16[ka:])


def _out1_kernel(x_ref, a_ref, w_ref, o_ref):
    o_ref[...] = x_ref[...] + _dot(a_ref[...], w_ref[...])


def _out_proj1(x2, a, w_out, tm):
    T, D = x2.shape
    ka = a.shape[1]
    return pl.pallas_call(
        _out1_kernel,
        grid=(T // tm,),
        in_specs=[pl.BlockSpec((tm, D), lambda i: (i, 0)), pl.BlockSpec((tm, ka), lambda i: (i, 0)),
                  pl.BlockSpec((ka, D), lambda i: (0, 0))],
        out_specs=pl.BlockSpec((tm, D), lambda i: (i, 0)),
        out_shape=jax.ShapeDtypeStruct((T, D), F32),
        compiler_params=_cparams(("parallel",)),
        name="out_proj1",
    )(x2, a, w_out.astype(BF16))


def _even_mix(x2, B, S, norm_g, w_in, conv_w, conv_b, ln_g, ln_b, gate_w2, gate_b, gla_norm_g, w_out):
    a_val, a_gate, q, v, r, glr, kt, glrt = _even_in(x2, norm_g, w_in, tm=512)
    y_a = _conformer_conv(a_val.reshape(B, S, CONV_CH), a_gate.reshape(B, S, CONV_CH), conv_w, conv_b, ln_g, ln_b, ts=512)
    y_b = _gla(q, v, r, glr, kt, glrt, gate_w2, gate_b, gla_norm_g, B, S, ts=256)
    return _out_proj2(x2, y_a.reshape(B * S, CONV_CH), y_b.reshape(B * S, GLA_HEADS * GLA_DV), w_out, tm=512)


def _fox_in_kernel(x_ref, g_ref, w_ref, wft_ref, fb_ref, qg_ref, kg_ref, q_ref, k_ref, v_ref, c_ref, carry_ref, *, ts):
    @pl.when(pl.program_id(1) == 0)
    def _():
        carry_ref[...] = jnp.zeros_like(carry_ref)

    xn = _rms(x_ref[0], g_ref[...]).astype(BF16)
    hd = FOX_HEADS * FOX_HD
    q = _dot(xn, w_ref[:, 0:hd])
    k = _dot(xn, w_ref[:, hd:2 * hd])
    v = _dot(xn, w_ref[:, 2 * hd:3 * hd])
    scale = FOX_HD ** -0.5
    for h in range(FOX_HEADS):
        hs = slice(h * FOX_HD, (h + 1) * FOX_HD)
        q_ref[0, h] = (_rms(q[:, hs], qg_ref[...]) * scale).astype(q_ref.dtype)
        k_ref[0, h] = _rms(k[:, hs], kg_ref[...]).astype(k_ref.dtype)
        v_ref[0, h] = v[:, hs].astype(v_ref.dtype)
    lf = _log_sigmoid(_dot_nt(wft_ref[...], xn) + fb_ref[...])
    row = lax.broadcasted_iota(jnp.int32, (ts, ts), 0)
    col = lax.broadcasted_iota(jnp.int32, (ts, ts), 1)
    c = _dot(lf, (row <= col).astype(F32), precision=HIGHEST) + carry_ref[...]
    c_ref[0] = c
    carry_ref[...] = c[:, ts - 1:ts]


def _fox_in(x2, B, S, g, w_in, fgate_b, q_g, k_g, ts):
    D = x2.shape[1]
    hd = FOX_HEADS * FOX_HD
    wb = w_in.astype(BF16)
    w_tok = wb[:, :3 * hd]
    wft = wb[:, 3 * hd:].T
    head_shape = jax.ShapeDtypeStruct((B, FOX_HEADS, S, FOX_HD), BF16)
    head_spec = pl.BlockSpec((1, FOX_HEADS, ts, FOX_HD), lambda b, i: (b, 0, i, 0))
    full = lambda shp: pl.BlockSpec(shp, lambda b, i: (0,) * len(shp))
    return pl.pallas_call(
        functools.partial(_fox_in_kernel, ts=ts),
        grid=(B, S // ts),
        in_specs=[pl.BlockSpec((1, ts, D), lambda b, i: (b, i, 0)), full((1, D)), full(w_tok.shape), full(wft.shape),
                  full((FOX_HEADS, 1)), full((1, FOX_HD)), full((1, FOX_HD))],
        out_specs=[head_spec, head_spec, head_spec, pl.BlockSpec((1, FOX_HEADS, ts), lambda b, i: (b, 0, i))],
        out_shape=[head_shape, head_shape, head_shape, jax.ShapeDtypeStruct((B, FOX_HEADS, S), F32)],
        scratch_shapes=[pltpu.VMEM((FOX_HEADS, 1), F32)],
        compiler_params=_cparams(("parallel", "arbitrary")),
        name="fox_in",
    )(x2.reshape(B, S, D), g.reshape(1, D), w_tok, wft, fgate_b.reshape(FOX_HEADS, 1), q_g.reshape(1, FOX_HD),
      k_g.reshape(1, FOX_HD))


FOX_PAIR = 2


FOX_EXP_ZERO = 110.0


def _fox_attn_kernel(bound_ref, q_ref, k_ref, v_ref, c_ref, o_ref, *, tq):
    qi = pl.program_id(2)
    row = lax.broadcasted_iota(jnp.int32, (tq, tq), 0)
    col = lax.broadcasted_iota(jnp.int32, (tq, tq), 1)
    causal = row >= col
    nt = c_ref.shape[3]
    tile_id = lax.broadcasted_iota(jnp.int32, (nt, 1), 0)
    outs = []
    for hh in range(FOX_PAIR):
        q = q_ref[0, hh]
        c_q = jnp.sum(jnp.where(tile_id == qi, c_ref[0, 0, hh, :, 0:1], 0.0), axis=0, keepdims=True)
        c_k = c_ref[0, 0, hh, :, tq - 1:tq]
        dead = (tile_id < qi) & (2.0 * bound_ref[0, 0] + c_q - c_k < -FOX_EXP_ZERO)
        lo = jnp.sum(dead.astype(jnp.int32))

        def step(ki, carry, masked):
            m, l, acc = carry
            k0 = pl.multiple_of(ki * tq, tq)
            k = k_ref[0, hh, pl.ds(k0, tq), :]
            v = v_ref[0, hh, pl.ds(k0, tq), :]
            s = _dot_nt(q, k) - c_ref[0, 0, hh, pl.ds(ki, 1), :]
            if masked:
                s = jnp.where(causal, s, -jnp.inf)
            m_new = jnp.maximum(m, jnp.max(s, axis=-1, keepdims=True))
            p = jnp.exp(s - m_new)
            alpha = jnp.exp(m - m_new)
            l = alpha * l + jnp.sum(p, axis=-1, keepdims=True)
            acc = alpha * acc + _dot(p.astype(BF16), v)
            return m_new, l, acc

        init = (jnp.full((tq, 1), -jnp.inf, F32), jnp.zeros((tq, 1), F32), jnp.zeros((tq, FOX_HD), F32))
        carry = lax.fori_loop(lo, qi, functools.partial(step, masked=False), init)
        _, l, acc = step(qi, carry, True)
        outs.append(acc / l)
    o_ref[0] = jnp.concatenate(outs, axis=-1).astype(o_ref.dtype)


def _fox_attn(q, k, v, c, bound, B, S, tq):
    npair = FOX_HEADS // FOX_PAIR
    c5 = c.reshape(B, npair, FOX_PAIR, S // tq, tq)
    qspec = pl.BlockSpec((1, FOX_PAIR, tq, FOX_HD), lambda b, h, i: (b, h, i, 0))
    kvspec = pl.BlockSpec((1, FOX_PAIR, S, FOX_HD), lambda b, h, i: (b, h, 0, 0))
    return pl.pallas_call(
        functools.partial(_fox_attn_kernel, tq=tq),
        grid=(B, npair, S // tq),
        in_specs=[pl.BlockSpec(memory_space=pltpu.SMEM), qspec, kvspec, kvspec,
                  pl.BlockSpec((1, 1, FOX_PAIR, S // tq, tq), lambda b, h, i: (b, h, 0, 0, 0))],
        out_specs=pl.BlockSpec((1, tq, FOX_PAIR * FOX_HD), lambda b, h, i: (b, i, h)),
        out_shape=jax.ShapeDtypeStruct((B, S, FOX_HEADS * FOX_HD), BF16),
        compiler_params=_cparams(("parallel", "parallel", "arbitrary")),
        name="fox_attn",
    )(bound, q, k, v, c5)


FOX_NORM_SLACK = 1.025


def _fox_mix(x2, B, S, norm_g, w_in, fgate_b, q_g, k_g, w_out):
    q, k, v, c = _fox_in(x2, B, S, norm_g, w_in, fgate_b, q_g, k_g, ts=512)
    bound = (FOX_NORM_SLACK * FOX_HD ** 0.5 * jnp.max(jnp.abs(q_g)) * jnp.max(jnp.abs(k_g))).reshape(1, 1).astype(F32)
    o = _fox_attn(q, k, v, c, bound, B, S, tq=512)
    return _out_proj1(x2, o.reshape(B * S, FOX_HEADS * FOX_HD), w_out, tm=512)


_PEER_CELLS = tuple((r, c) for r in range(PEER_TOPK) for c in range(PEER_TOPK) if (r + 1) * (c + 1) <= PEER_TOPK)


def _peer_route_kernel(x_ref, g_ref, wqt_ref, keys_ref, xnt_ref, na_ref, e1_ref, rk_ref, e2_ref,
                       s_ref, val_ref, idx_ref, nrow_ref, invz_ref):
    tm = x_ref.shape[0]
    xn = _rms(x_ref[...], g_ref[...])
    xnt_ref[...] = xn.T.astype(xnt_ref.dtype)
    qt = _dot_nt(wqt_ref[...], xn.astype(BF16)).astype(BF16)
    for j in range(2 * PEER_HEADS):
        s_ref[j] = _dot(keys_ref[j], qt[j * PEER_HALF:(j + 1) * PEER_HALF, :])
    kio = lax.broadcasted_iota(jnp.int32, (PEER_NKEYS, tm), 0).astype(F32)
    nkeys = float(PEER_NKEYS)

    def extract(j, carry):
        h = j // 2
        half = j % 2
        work = s_ref[j]
        for r in range(PEER_TOPK):
            m = jnp.max(work, axis=0, keepdims=True)
            idx = jnp.min(jnp.where(work == m, kio, nkeys), axis=0, keepdims=True)
            val_ref[half, r, pl.ds(h, 1), :] = m
            idx_ref[half, r, pl.ds(h, 1), :] = idx
            if r + 1 < PEER_TOPK:
                work = jnp.where(kio == idx, -jnp.inf, work)
        return carry

    lax.fori_loop(0, 2 * PEER_HEADS, extract, 0)

    v1 = [val_ref[0, r] for r in range(PEER_TOPK)]
    v2 = [val_ref[1, c] for c in range(PEER_TOPK)]
    cand = [v1[r] + v2[c] for r, c in _PEER_CELLS]
    flat = [r * PEER_TOPK + c for r, c in _PEER_CELLS]
    big = PEER_TOPK * PEER_TOPK
    sel_val, sel_row = [], []
    for j in range(PEER_TOPK):
        m = functools.reduce(jnp.maximum, cand)
        idx = functools.reduce(jnp.minimum, [jnp.where(cv == m, f, big) for cv, f in zip(cand, flat)])
        sel_val.append(m)
        sel_row.append(idx >> 4)
        if j + 1 < PEER_TOPK:
            cand = [jnp.where(idx == f, -jnp.inf, cv) for cv, f in zip(cand, flat)]
    z = functools.reduce(jnp.add, [jnp.exp(m - sel_val[0]) for m in sel_val])
    invz_ref[...] = 1.0 / z
    for r in range(PEER_TOPK):
        nrow_ref[r] = functools.reduce(jnp.add, [jnp.where(sr == r, 1.0, 0.0) for sr in sel_row])

    def emit(h, carry):
        row = pl.ds(h, 1)
        na = jnp.zeros((PEER_NKEYS, tm), F32)
        for r in range(PEER_TOPK):
            na = na + jnp.where(kio == idx_ref[0, r, row, :], nrow_ref[r, row, :], 0.0)
        rk = jnp.full((PEER_NKEYS, tm), float(PEER_TOPK), F32)
        for c in range(PEER_TOPK):
            rk = rk - jnp.where(kio == idx_ref[1, c, row, :], float(PEER_TOPK - c), 0.0)
        na_ref[h] = na
        rk_ref[h] = rk.astype(rk_ref.dtype)
        e1_ref[h] = jnp.exp(s_ref[2 * h] - val_ref[0, 0, row, :]) * invz_ref[row, :]
        e2_ref[h] = jnp.exp(s_ref[2 * h + 1] - val_ref[1, 0, row, :]).astype(e2_ref.dtype)
        return carry

    lax.fori_loop(0, PEER_HEADS, emit, 0)


def _peer_route(x2, g, wq, keys, tm):
    T, D = x2.shape
    nq = wq.shape[1]
    wqt = wq.astype(BF16).T
    keys2 = keys.reshape(2 * PEER_HEADS, PEER_NKEYS, PEER_HALF).astype(BF16)
    per_key = lambda dt: jax.ShapeDtypeStruct((PEER_HEADS, PEER_NKEYS, T), dt)
    key_spec = pl.BlockSpec((PEER_HEADS, PEER_NKEYS, tm), lambda i: (0, 0, i))
    rank_rows = pltpu.VMEM((2, PEER_TOPK, PEER_HEADS, tm), F32)
    return pl.pallas_call(
        _peer_route_kernel,
        grid=(T // tm,),
        in_specs=[pl.BlockSpec((tm, D), lambda i: (i, 0)), pl.BlockSpec((1, D), lambda i: (0, 0)),
                  pl.BlockSpec((nq, D), lambda i: (0, 0)),
                  pl.BlockSpec(keys2.shape, lambda i: (0, 0, 0))],
        out_specs=[pl.BlockSpec((D, tm), lambda i: (0, i)), key_spec, key_spec, key_spec, key_spec],
        out_shape=[jax.ShapeDtypeStruct((D, T), BF16), per_key(F32), per_key(F32), per_key(BF16), per_key(BF16)],
        scratch_shapes=[pltpu.VMEM((2 * PEER_HEADS, PEER_NKEYS, tm), F32), rank_rows, rank_rows,
                        pltpu.VMEM((PEER_TOPK, PEER_HEADS, tm), F32), pltpu.VMEM((PEER_HEADS, tm), F32)],
        compiler_params=_cparams(("parallel",)),
        name="peer_route",
    )(x2, g.reshape(1, D), wqt, keys2)


PEER_ABLK = 8


def _gelu(x):
    return 0.5 * x * (1.0 + lax.erf(x * (2.0 ** -0.5)))


def _peer_dense_kernel(xnt_ref, u_ref, vt_ref, sel_ref, na_ref, e1_ref, rk_ref, e2_ref, x_ref, o_ref,
                       acc_ref, m_ref, rep_ref):
    j = pl.program_id(1)

    @pl.when(j == 0)
    def _():
        acc_ref[...] = jnp.zeros_like(acc_ref)

    tt = m_ref.shape[1]
    zero = jnp.zeros((BF16_ROWS, tt), BF16)
    npair = PEER_HEADS * PEER_ABLK
    rows = jnp.concatenate([na_ref[...].reshape(npair, tt), e1_ref[...].reshape(npair, tt)], axis=0).astype(BF16)
    rep_ref[...] = _dot(sel_ref[...], rows).astype(BF16)
    for al in range(PEER_ABLK):
        for g in range(PEER_NKEYS // BF16_ROWS):
            bs = slice(g * BF16_ROWS, (g + 1) * BF16_ROWS)
            w = None
            for h in range(PEER_HEADS):
                r_n = (h * PEER_ABLK + al) * BF16_ROWS
                r_e = (npair + h * PEER_ABLK + al) * BF16_ROWS
                e2m = jnp.where(rk_ref[h, bs, :] < rep_ref[r_n:r_n + BF16_ROWS, :], e2_ref[h, bs, :], zero)
                wh = e2m * rep_ref[r_e:r_e + BF16_ROWS, :]
                w = wh if w is None else w + wh
            m_ref[al * PEER_NKEYS + g * BF16_ROWS:al * PEER_NKEYS + (g + 1) * BF16_ROWS, :] = w
    act = _gelu(_dot(u_ref[...], xnt_ref[...]).astype(BF16))
    m_ref[...] = m_ref[...] * act
    acc_ref[...] += _dot(vt_ref[...], m_ref[...])

    @pl.when(j == pl.num_programs(1) - 1)
    def _():
        o_ref[...] = x_ref[...] + acc_ref[...].T


def _peer_dense(x2, xnt, u, v, na, e1, rk, e2, tt):
    T, D = x2.shape
    ne = u.shape[0]
    eb = PEER_ABLK * PEER_NKEYS
    ub = u.astype(BF16)
    vt = v.astype(BF16).T
    nrep = 2 * PEER_HEADS * PEER_ABLK
    sel = (jnp.arange(nrep * BF16_ROWS)[:, None] // BF16_ROWS == jnp.arange(nrep)[None, :]).astype(BF16)
    key_a = pl.BlockSpec((PEER_HEADS, PEER_ABLK, tt), lambda i, j: (0, j, i))
    key_b = pl.BlockSpec((PEER_HEADS, PEER_NKEYS, tt), lambda i, j: (0, 0, i))
    return pl.pallas_call(
        _peer_dense_kernel,
        grid=(T // tt, ne // eb),
        in_specs=[pl.BlockSpec((D, tt), lambda i, j: (0, i)),
                  pl.BlockSpec((eb, D), lambda i, j: (j, 0)),
                  pl.BlockSpec((D, eb), lambda i, j: (0, j)),
                  pl.BlockSpec(sel.shape, lambda i, j: (0, 0)),
                  key_a, key_a, key_b, key_b,
                  pl.BlockSpec((tt, D), lambda i, j: (i, 0))],
        out_specs=pl.BlockSpec((tt, D), lambda i, j: (i, 0)),
        out_shape=jax.ShapeDtypeStruct((T, D), F32),
        scratch_shapes=[pltpu.VMEM((D, tt), F32), pltpu.VMEM((eb, tt), BF16),
                        pltpu.VMEM((nrep * BF16_ROWS, tt), BF16)],
        compiler_params=_cparams(("parallel", "arbitrary")),
        name="peer_dense",
    )(xnt, ub, vt, sel, na, e1, rk, e2, x2)


def _peer(x2, norm_g, wq, keys, u, v):
    T, D = x2.shape
    xnt, na, e1, rk, e2 = _peer_route(x2, norm_g, wq, keys, tm=512)
    return _peer_dense(x2, xnt, u, v, na, e1, rk, e2, tt=512)


def kernel(x, ev_norm_mix, ev_w_in, ev_conv_w, ev_conv_b, ev_conv_ln_g, ev_conv_ln_b, ev_gate_w2, ev_gate_b,
           ev_gla_norm_g, ev_w_out, od_norm_mix, od_w_in, od_fgate_b, od_q_norm_g, od_k_norm_g, od_w_out,
           ffn_norm, peer_wq, peer_keys, peer_u, peer_v):
    B, S, D = x.shape
    x2 = x.reshape(B * S, D)
    x2 = _even_mix(x2, B, S, ev_norm_mix[0], ev_w_in[0], ev_conv_w[0], ev_conv_b[0], ev_conv_ln_g[0],
                   ev_conv_ln_b[0], ev_gate_w2[0], ev_gate_b[0], ev_gla_norm_g[0], ev_w_out[0])
    x2 = _peer(x2, ffn_norm[0], peer_wq[0], peer_keys[0], peer_u[0], peer_v[0])
    x2 = _fox_mix(x2, B, S, od_norm_mix[0], od_w_in[0], od_fgate_b[0], od_q_norm_g[0], od_k_norm_g[0], od_w_out[0])
    x2 = _peer(x2, ffn_norm[1], peer_wq[1], peer_keys[1], peer_u[1], peer_v[1])
    return x2.reshape(B, S, D)
```

```python
import functools

import jax
import jax.numpy as jnp
from jax import lax
from jax.experimental import pallas as pl
from jax.experimental.pallas import tpu as pltpu

F32 = jnp.float32
BF16 = jnp.bfloat16
HIGHEST = lax.Precision.HIGHEST

EPS = 1e-6
D_MODEL = 1024
CHUNK = 64
CONV_CH = 512
CONV_WIDTH = 31
GLA_HEADS = 4
GLA_DK = 64
GLA_DV = 128
GLA_GATE_RANK = 16
GLA_GATE_TAU = 16.0
FOX_HEADS = 16
FOX_HD = 64
PEER_HEADS = 8
PEER_NKEYS = 128
PEER_TOPK = 16
PEER_HALF = 128

LANES = 128
BF16_ROWS = 16
VMEM_LIMIT = 56 * 1024 * 1024


def _cparams(sem):
    return pltpu.CompilerParams(dimension_semantics=sem, vmem_limit_bytes=VMEM_LIMIT)


def _rms(x, g):
    ms = jnp.mean(x * x, axis=-1, keepdims=True)
    return x * lax.rsqrt(ms + EPS) * g


def _log_sigmoid(z):
    return jnp.minimum(z, 0.0) - jnp.log(1.0 + jnp.exp(-jnp.abs(z)))


def _dot(a, b, **kw):
    return jnp.dot(a, b, preferred_element_type=F32, **kw)


def _dot_nt(a, b, **kw):
    return lax.dot_general(a, b, (((1,), (1,)), ((), ())), preferred_element_type=F32, **kw)


def _even_in_kernel(x_ref, g_ref, w_ref, wkt_ref, wgt_ref,
                    aval_ref, agate_ref, q_ref, v_ref, r_ref, glr_ref, kt_ref, glrt_ref):
    xn = _rms(x_ref[...], g_ref[...]).astype(BF16)
    off = 0
    for o_ref in (aval_ref, agate_ref, q_ref, v_ref, r_ref, glr_ref):
        n = o_ref.shape[-1]
        o_ref[...] = _dot(xn, w_ref[:, off:off + n]).astype(o_ref.dtype)
        off += n
    kt_ref[...] = _dot_nt(wkt_ref[...], xn)
    glrt_ref[...] = _dot_nt(wgt_ref[...], xn)


def _even_in(x2, g, w_in, tm):
    T, D = x2.shape
    hk = GLA_HEADS * GLA_DK
    hv = GLA_HEADS * GLA_DV
    c0, c1, c2, c3, c4, c5 = CONV_CH, 2 * CONV_CH, 2 * CONV_CH + hk, 2 * CONV_CH + 2 * hk, 2 * CONV_CH + 2 * hk + hv, 2 * CONV_CH + 2 * hk + 2 * hv
    wb = w_in.astype(BF16)
    w_tok = jnp.concatenate([wb[:, :c2], wb[:, c3:]], axis=1)
    wkt = wb[:, c2:c3].T
    wgt = wb[:, c5:].T
    widths = (CONV_CH, CONV_CH, hk, hv, hv, GLA_GATE_RANK)
    out_shape = [jax.ShapeDtypeStruct((T, n), F32) for n in widths]
    out_shape += [jax.ShapeDtypeStruct((hk, T), F32), jax.ShapeDtypeStruct((GLA_GATE_RANK, T), F32)]
    out_specs = [pl.BlockSpec((tm, n), lambda i: (i, 0)) for n in widths]
    out_specs += [pl.BlockSpec((hk, tm), lambda i: (0, i)), pl.BlockSpec((GLA_GATE_RANK, tm), lambda i: (0, i))]
    return pl.pallas_call(
        _even_in_kernel,
        grid=(T // tm,),
        in_specs=[
            pl.BlockSpec((tm, D), lambda i: (i, 0)),
            pl.BlockSpec((1, D), lambda i: (0, 0)),
            pl.BlockSpec(w_tok.shape, lambda i: (0, 0)),
            pl.BlockSpec(wkt.shape, lambda i: (0, 0)),
            pl.BlockSpec(wgt.shape, lambda i: (0, 0)),
        ],
        out_specs=out_specs,
        out_shape=out_shape,
        compiler_params=_cparams(("parallel",)),
        name="even_in",
    )(x2, g.reshape(1, D), w_tok, wkt, wgt)


CONV_HALO = 32
CONV_ROWS = 64


def _conv_kernel(val_ref, gate_ref, hval_ref, hgate_ref, w_ref, b_ref, lg_ref, lb_ref, o_ref, ext_ref, *, ts):
    first = pl.program_id(1) == 0
    u_halo = hval_ref[0] * jax.nn.sigmoid(hgate_ref[0])
    ext_ref[0:CONV_HALO, :] = jnp.where(first, 0.0, u_halo)
    ext_ref[CONV_HALO:CONV_HALO + ts, :] = val_ref[0] * jax.nn.sigmoid(gate_ref[0])
    lead = CONV_HALO - (CONV_WIDTH - 1)
    for r0 in range(0, ts, CONV_ROWS):
        acc = jnp.broadcast_to(b_ref[...], (CONV_ROWS, CONV_CH))
        for k in range(CONV_WIDTH):
            acc = acc + w_ref[k:k + 1, :] * ext_ref[r0 + lead + k:r0 + lead + k + CONV_ROWS, :]
        mu = jnp.mean(acc, axis=-1, keepdims=True)
        xc = acc - mu
        y = xc * lax.rsqrt(jnp.mean(xc * xc, axis=-1, keepdims=True) + EPS) * lg_ref[...] + lb_ref[...]
        o_ref[0, r0:r0 + CONV_ROWS, :] = (y * jax.nn.sigmoid(y)).astype(o_ref.dtype)


def _conformer_conv(val, gate, conv_w, conv_b, ln_g, ln_b, ts):
    B, S, C = val.shape
    hb = ts // CONV_HALO
    cur = pl.BlockSpec((1, ts, C), lambda b, i: (b, i, 0))
    halo = pl.BlockSpec((1, CONV_HALO, C), lambda b, i: (b, jnp.maximum(i * hb - 1, 0), 0))
    vec = pl.BlockSpec((1, C), lambda b, i: (0, 0))
    return pl.pallas_call(
        functools.partial(_conv_kernel, ts=ts),
        grid=(B, S // ts),
        in_specs=[cur, cur, halo, halo, pl.BlockSpec((CONV_WIDTH, C), lambda b, i: (0, 0)), vec, vec, vec],
        out_specs=pl.BlockSpec((1, ts, C), lambda b, i: (b, i, 0)),
        out_shape=jax.ShapeDtypeStruct((B, S, C), BF16),
        scratch_shapes=[pltpu.VMEM((CONV_HALO + ts, C), F32)],
        compiler_params=_cparams(("parallel", "parallel")),
        name="conformer_conv",
    )(val, gate, val, gate, conv_w, conv_b.reshape(1, C), ln_g.reshape(1, C), ln_b.reshape(1, C))


def _gla_kernel(q_ref, v_ref, r_ref, glr_ref, kt_ref, glrt_ref, w2_ref, w2t_ref, gb_ref, gbt_ref, ng_ref,
                o_ref, state_ref, *, ts):
    @pl.when(pl.program_id(1) == 0)
    def _():
        state_ref[...] = jnp.zeros_like(state_ref)

    L = CHUNK
    row = lax.broadcasted_iota(jnp.int32, (L, L), 0)
    col = lax.broadcasted_iota(jnp.int32, (L, L), 1)
    causal = row >= col
    tril = causal.astype(F32)
    triu = (row <= col).astype(F32)
    mid = L // 2 - 1
    scale = GLA_DK ** -0.5

    la = _log_sigmoid(_dot(glr_ref[0], w2_ref[...], precision=HIGHEST) + gb_ref[...]) * (1.0 / GLA_GATE_TAU)
    lat = _log_sigmoid(_dot(w2t_ref[...], glrt_ref[...], precision=HIGHEST) + gbt_ref[...]) * (1.0 / GLA_GATE_TAU)

    for c in range(ts // L):
        sl = slice(c * L, (c + 1) * L)
        b = _dot(tril, la[sl, :], precision=HIGHEST)
        bt = _dot(lat[:, sl], triu, precision=HIGHEST)
        qc = q_ref[0, sl, :] * scale
        q_in = (qc * jnp.exp(b)).astype(BF16)
        q_a = (qc * jnp.exp(b - b[mid:mid + 1, :])).astype(BF16)
        kt = kt_ref[:, sl]
        bt_last = bt[:, L - 1:L]
        k_a = (kt * jnp.exp(bt[:, mid:mid + 1] - bt)).astype(BF16)
        k_dec = (kt * jnp.exp(bt_last - bt)).astype(BF16)
        v_c = v_ref[0, sl, :].astype(BF16)
        r_c = r_ref[0, sl, :]
        for h in range(GLA_HEADS):
            ks = slice(h * GLA_DK, (h + 1) * GLA_DK)
            vs = slice(h * GLA_DV, (h + 1) * GLA_DV)
            state = state_ref[h]
            attn = jnp.where(causal, _dot(q_a[:, ks], k_a[ks, :]), 0.0)
            o = _dot(q_in[:, ks], state.astype(BF16)) + _dot(attn.astype(BF16), v_c[:, vs])
            state_ref[h] = state * jnp.exp(bt_last[ks, :]) + _dot(k_dec[ks, :], v_c[:, vs])
            y = _rms(o, ng_ref[...])
            rg = r_c[:, vs]
            o_ref[0, sl, vs] = (y * (rg * jax.nn.sigmoid(rg))).astype(o_ref.dtype)


def _gla(q, v, r, glr, kt, glrt, gate_w2, gate_b, norm_g, B, S, ts):
    hk = GLA_HEADS * GLA_DK
    hv = GLA_HEADS * GLA_DV
    nt = S // ts
    tok = lambda n: pl.BlockSpec((1, ts, n), lambda b, i: (b, i, 0))
    tr = lambda n: pl.BlockSpec((n, ts), lambda b, i: (0, b * nt + i))
    full = lambda shp: pl.BlockSpec(shp, lambda b, i: (0,) * len(shp))
    return pl.pallas_call(
        functools.partial(_gla_kernel, ts=ts),
        grid=(B, nt),
        in_specs=[tok(hk), tok(hv), tok(hv), tok(GLA_GATE_RANK), tr(hk), tr(GLA_GATE_RANK),
                  full((GLA_GATE_RANK, hk)), full((hk, GLA_GATE_RANK)), full((1, hk)), full((hk, 1)), full((1, GLA_DV))],
        out_specs=tok(hv),
        out_shape=jax.ShapeDtypeStruct((B, S, hv), BF16),
        scratch_shapes=[pltpu.VMEM((GLA_HEADS, GLA_DK, GLA_DV), F32)],
        compiler_params=_cparams(("parallel", "arbitrary")),
        name="gla",
    )(q.reshape(B, S, hk), v.reshape(B, S, hv), r.reshape(B, S, hv), glr.reshape(B, S, GLA_GATE_RANK), kt, glrt,
      gate_w2, gate_w2.T, gate_b.reshape(1, hk), gate_b.reshape(hk, 1), norm_g.reshape(1, GLA_DV))


def _out2_kernel(x_ref, a_ref, b_ref, wa_ref, wb_ref, o_ref):
    o_ref[...] = x_ref[...] + _dot(a_ref[...], wa_ref[...]) + _dot(b_ref[...], wb_ref[...])


def _out_proj2(x2, a, b, w_out, tm):
    T, D = x2.shape
    ka, kb = a.shape[1], b.shape[1]
    wb16 = w_out.astype(BF16)
    return pl.pallas_call(
        _out2_kernel,
        grid=(T // tm,),
        in_specs=[pl.BlockSpec((tm, D), lambda i: (i, 0)), pl.BlockSpec((tm, ka), lambda i: (i, 0)),
                  pl.BlockSpec((tm, kb), lambda i: (i, 0)), pl.BlockSpec((ka, D), lambda i: (0, 0)),
                  pl.BlockSpec((kb, D), lambda i: (0, 0))],
        out_specs=pl.BlockSpec((tm, D), lambda i: (i, 0)),
        out_shape=jax.ShapeDtypeStruct((T, D), F32),
        compiler_params=_cparams(("parallel",)),
        name="out_proj2",
    )(x2, a, b, wb16[:ka], wb16[ka:])


def _out1_kernel(x_ref, a_ref, w_ref, o_ref):
    o_ref[...] = x_ref[...] + _dot(a_ref[...], w_ref[...])


def _out_proj1(x2, a, w_out, tm):
    T, D = x2.shape
    ka = a.shape[1]
    return pl.pallas_call(
        _out1_kernel,
        grid=(T // tm,),
        in_specs=[pl.BlockSpec((tm, D), lambda i: (i, 0)), pl.BlockSpec((tm, ka), lambda i: (i, 0)),
                  pl.BlockSpec((ka, D), lambda i: (0, 0))],
        out_specs=pl.BlockSpec((tm, D), lambda i: (i, 0)),
        out_shape=jax.ShapeDtypeStruct((T, D), F32),
        compiler_params=_cparams(("parallel",)),
        name="out_proj1",
    )(x2, a, w_out.astype(BF16))


def _even_mix(x2, B, S, norm_g, w_in, conv_w, conv_b, ln_g, ln_b, gate_w2, gate_b, gla_norm_g, w_out):
    a_val, a_gate, q, v, r, glr, kt, glrt = _even_in(x2, norm_g, w_in, tm=512)
    y_a = _conformer_conv(a_val.reshape(B, S, CONV_CH), a_gate.reshape(B, S, CONV_CH), conv_w, conv_b, ln_g, ln_b, ts=512)
    y_b = _gla(q, v, r, glr, kt, glrt, gate_w2, gate_b, gla_norm_g, B, S, ts=256)
    return _out_proj2(x2, y_a.reshape(B * S, CONV_CH), y_b.reshape(B * S, GLA_HEADS * GLA_DV), w_out, tm=512)


def _fox_in_kernel(x_ref, g_ref, w_ref, wft_ref, fb_ref, qg_ref, kg_ref, hm_ref, q_ref, k_ref, v_ref, c_ref,
                   carry_ref, *, ts):
    @pl.when(pl.program_id(1) == 0)
    def _():
        carry_ref[...] = jnp.zeros_like(carry_ref)

    xn = _rms(x_ref[0], g_ref[...]).astype(BF16)
    hd = FOX_HEADS * FOX_HD

    def head_rms(t, gain):
        ms = _dot((t * t).astype(BF16), hm_ref[...])
        return (t * lax.rsqrt(ms + EPS) * gain)

    q_ref[0] = head_rms(_dot(xn, w_ref[:, 0:hd]), qg_ref[...]).astype(q_ref.dtype)
    k_ref[0] = head_rms(_dot(xn, w_ref[:, hd:2 * hd]), kg_ref[...]).astype(k_ref.dtype)
    v_ref[0] = _dot(xn, w_ref[:, 2 * hd:3 * hd]).astype(v_ref.dtype)
    lf = _log_sigmoid(_dot_nt(wft_ref[...], xn) + fb_ref[...])
    row = lax.broadcasted_iota(jnp.int32, (ts, ts), 0)
    col = lax.broadcasted_iota(jnp.int32, (ts, ts), 1)
    c = _dot(lf, (row <= col).astype(F32), precision=HIGHEST) + carry_ref[...]
    c_ref[0] = c
    carry_ref[...] = c[:, ts - 1:ts]


def _fox_in(x2, B, S, g, w_in, fgate_b, q_g, k_g, ts):
    D = x2.shape[1]
    hd = FOX_HEADS * FOX_HD
    wb = w_in.astype(BF16)
    w_tok = wb[:, :3 * hd]
    wft = wb[:, 3 * hd:].T
    head_shape = jax.ShapeDtypeStruct((B, S, hd), BF16)
    head_spec = pl.BlockSpec((1, ts, hd), lambda b, i: (b, i, 0))
    full = lambda shp: pl.BlockSpec(shp, lambda b, i: (0,) * len(shp))
    head_of = jnp.arange(hd) // FOX_HD
    head_mean = jnp.where(head_of[:, None] == head_of[None, :], 1.0 / FOX_HD, 0.0).astype(BF16)
    q_gain = jnp.tile(q_g * FOX_HD ** -0.5, FOX_HEADS).reshape(1, hd)
    k_gain = jnp.tile(k_g, FOX_HEADS).reshape(1, hd)
    return pl.pallas_call(
        functools.partial(_fox_in_kernel, ts=ts),
        grid=(B, S // ts),
        in_specs=[pl.BlockSpec((1, ts, D), lambda b, i: (b, i, 0)), full((1, D)), full(w_tok.shape), full(wft.shape),
                  full((FOX_HEADS, 1)), full((1, hd)), full((1, hd)), full((hd, hd))],
        out_specs=[head_spec, head_spec, head_spec, pl.BlockSpec((1, FOX_HEADS, ts), lambda b, i: (b, 0, i))],
        out_shape=[head_shape, head_shape, head_shape, jax.ShapeDtypeStruct((B, FOX_HEADS, S), F32)],
        scratch_shapes=[pltpu.VMEM((FOX_HEADS, 1), F32)],
        compiler_params=_cparams(("parallel", "arbitrary")),
        name="fox_in",
    )(x2.reshape(B, S, D), g.reshape(1, D), w_tok, wft, fgate_b.reshape(FOX_HEADS, 1), q_gain, k_gain, head_mean)


FOX_PAIR = 2


FOX_EXP_ZERO = 110.0


def _fox_attn_kernel(bound_ref, q_ref, kin_ref, vin_ref, c_ref, o_ref, k_ref, v_ref, *, tq):
    qi = pl.program_id(2)

    @pl.when(qi == 0)
    def _():
        for hh in range(FOX_PAIR):
            hs = slice(hh * FOX_HD, (hh + 1) * FOX_HD)
            k_ref[hh] = kin_ref[0, :, hs]
            v_ref[hh] = vin_ref[0, :, hs]

    row = lax.broadcasted_iota(jnp.int32, (tq, tq), 0)
    col = lax.broadcasted_iota(jnp.int32, (tq, tq), 1)
    causal = row >= col
    nt = c_ref.shape[3]
    tile_id = lax.broadcasted_iota(jnp.int32, (nt, 1), 0)
    outs = []
    for hh in range(FOX_PAIR):
        q = q_ref[0, :, hh * FOX_HD:(hh + 1) * FOX_HD]
        c_q = jnp.sum(jnp.where(tile_id == qi, c_ref[0, 0, hh, :, 0:1], 0.0), axis=0, keepdims=True)
        c_k = c_ref[0, 0, hh, :, tq - 1:tq]
        dead = (tile_id < qi) & (2.0 * bound_ref[0, 0] + c_q - c_k < -FOX_EXP_ZERO)
        lo = jnp.sum(dead.astype(jnp.int32))

        def step(ki, carry, masked):
            m, l, acc = carry
            k0 = pl.multiple_of(ki * tq, tq)
            k = k_ref[hh, pl.ds(k0, tq), :]
            v = v_ref[hh, pl.ds(k0, tq), :]
            s = _dot_nt(q, k) - c_ref[0, 0, hh, pl.ds(ki, 1), :]
            if masked:
                s = jnp.where(causal, s, -jnp.inf)
            m_new = jnp.maximum(m, jnp.max(s, axis=-1, keepdims=True))
            p = jnp.exp(s - m_new)
            alpha = jnp.exp(m - m_new)
            l = alpha * l + jnp.sum(p, axis=-1, keepdims=True)
            acc = alpha * acc + _dot(p.astype(BF16), v)
            return m_new, l, acc

        init = (jnp.full((tq, 1), -jnp.inf, F32), jnp.zeros((tq, 1), F32), jnp.zeros((tq, FOX_HD), F32))
        carry = lax.fori_loop(lo, qi, functools.partial(step, masked=False), init)
        _, l, acc = step(qi, carry, True)
        outs.append(acc / l)
    o_ref[0] = jnp.concatenate(outs, axis=-1).astype(o_ref.dtype)


def _fox_attn(q, k, v, c, bound, B, S, tq):
    npair = FOX_HEADS // FOX_PAIR
    c5 = c.reshape(B, npair, FOX_PAIR, S // tq, tq)
    pw = FOX_PAIR * FOX_HD
    qspec = pl.BlockSpec((1, tq, pw), lambda b, h, i: (b, i, h))
    kvspec = pl.BlockSpec((1, S, pw), lambda b, h, i: (b, 0, h))
    return pl.pallas_call(
        functools.partial(_fox_attn_kernel, tq=tq),
        grid=(B, npair, S // tq),
        in_specs=[pl.BlockSpec(memory_space=pltpu.SMEM), qspec, kvspec, kvspec,
                  pl.BlockSpec((1, 1, FOX_PAIR, S // tq, tq), lambda b, h, i: (b, h, 0, 0, 0))],
        out_specs=pl.BlockSpec((1, tq, FOX_PAIR * FOX_HD), lambda b, h, i: (b, i, h)),
        out_shape=jax.ShapeDtypeStruct((B, S, FOX_HEADS * FOX_HD), BF16),
        scratch_shapes=[pltpu.VMEM((FOX_PAIR, S, FOX_HD), BF16), pltpu.VMEM((FOX_PAIR, S, FOX_HD), BF16)],
        compiler_params=_cparams(("parallel", "parallel", "arbitrary")),
        name="fox_attn",
    )(bound, q, k, v, c5)


FOX_NORM_SLACK = 1.025


def _fox_mix(x2, B, S, norm_g, w_in, fgate_b, q_g, k_g, w_out):
    q, k, v, c = _fox_in(x2, B, S, norm_g, w_in, fgate_b, q_g, k_g, ts=512)
    bound = (FOX_NORM_SLACK * FOX_HD ** 0.5 * jnp.max(jnp.abs(q_g)) * jnp.max(jnp.abs(k_g))).reshape(1, 1).astype(F32)
    o = _fox_attn(q, k, v, c, bound, B, S, tq=512)
    return _out_proj1(x2, o.reshape(B * S, FOX_HEADS * FOX_HD), w_out, tm=512)


_PEER_CELLS = tuple((r, c) for r in range(PEER_TOPK) for c in range(PEER_TOPK) if (r + 1) * (c + 1) <= PEER_TOPK)


def _peer_route_kernel(x_ref, g_ref, wqt_ref, keys_ref, xnt_ref, na_ref, e1_ref, rk_ref, e2_ref,
                       s_ref, val_ref, idx_ref, nrow_ref, invz_ref):
    tm = x_ref.shape[0]
    xn = _rms(x_ref[...], g_ref[...])
    xnt_ref[...] = xn.T.astype(xnt_ref.dtype)
    qt = _dot_nt(wqt_ref[...], xn.astype(BF16)).astype(BF16)
    for j in range(2 * PEER_HEADS):
        s_ref[j] = _dot(keys_ref[j], qt[j * PEER_HALF:(j + 1) * PEER_HALF, :])
    kio = lax.broadcasted_iota(jnp.int32, (PEER_NKEYS, tm), 0).astype(F32)
    nkeys = float(PEER_NKEYS)

    def extract(j, carry):
        h = j // 2
        half = j % 2
        work = s_ref[j]
        for r in range(PEER_TOPK):
            m = jnp.max(work, axis=0, keepdims=True)
            idx = jnp.min(jnp.where(work == m, kio, nkeys), axis=0, keepdims=True)
            val_ref[half, r, pl.ds(h, 1), :] = m
            idx_ref[half, r, pl.ds(h, 1), :] = idx
            if r + 1 < PEER_TOPK:
                work = jnp.where(kio == idx, -jnp.inf, work)
        return carry

    lax.fori_loop(0, 2 * PEER_HEADS, extract, 0)

    v1 = [val_ref[0, r] for r in range(PEER_TOPK)]
    v2 = [val_ref[1, c] for c in range(PEER_TOPK)]
    cand = [v1[r] + v2[c] for r, c in _PEER_CELLS]
    flat = [r * PEER_TOPK + c for r, c in _PEER_CELLS]
    big = PEER_TOPK * PEER_TOPK
    sel_val, sel_row = [], []
    for j in range(PEER_TOPK):
        m = functools.reduce(jnp.maximum, cand)
        idx = functools.reduce(jnp.minimum, [jnp.where(cv == m, f, big) for cv, f in zip(cand, flat)])
        sel_val.append(m)
        sel_row.append(idx >> 4)
        if j + 1 < PEER_TOPK:
            cand = [jnp.where(idx == f, -jnp.inf, cv) for cv, f in zip(cand, flat)]
    z = functools.reduce(jnp.add, [jnp.exp(m - sel_val[0]) for m in sel_val])
    invz_ref[...] = 1.0 / z
    for r in range(PEER_TOPK):
        nrow_ref[r] = functools.reduce(jnp.add, [jnp.where(sr == r, 1.0, 0.0) for sr in sel_row])

    def emit(h, carry):
        row = pl.ds(h, 1)
        na = jnp.zeros((PEER_NKEYS, tm), F32)
        for r in range(PEER_TOPK):
            na = jnp.where(kio == idx_ref[0, r, row, :], nrow_ref[r, row, :], na)
        rk = jnp.full((PEER_NKEYS, tm), float(PEER_TOPK), F32)
        for c in range(PEER_TOPK):
            rk = jnp.where(kio == idx_ref[1, c, row, :], float(c), rk)
        na_ref[h] = na
        rk_ref[h] = rk.astype(rk_ref.dtype)
        e1_ref[h] = jnp.exp(s_ref[2 * h] - val_ref[0, 0, row, :]) * invz_ref[row, :]
        e2_ref[h] = jnp.exp(s_ref[2 * h + 1] - val_ref[1, 0, row, :]).astype(e2_ref.dtype)
        return carry

    lax.fori_loop(0, PEER_HEADS, emit, 0)


def _peer_route(x2, g, wq, keys, tm):
    T, D = x2.shape
    nq = wq.shape[1]
    wqt = wq.astype(BF16).T
    keys2 = keys.reshape(2 * PEER_HEADS, PEER_NKEYS, PEER_HALF).astype(BF16)
    per_key = lambda dt: jax.ShapeDtypeStruct((PEER_HEADS, PEER_NKEYS, T), dt)
    key_spec = pl.BlockSpec((PEER_HEADS, PEER_NKEYS, tm), lambda i: (0, 0, i))
    rank_rows = pltpu.VMEM((2, PEER_TOPK, PEER_HEADS, tm), F32)
    return pl.pallas_call(
        _peer_route_kernel,
        grid=(T // tm,),
        in_specs=[pl.BlockSpec((tm, D), lambda i: (i, 0)), pl.BlockSpec((1, D), lambda i: (0, 0)),
                  pl.BlockSpec((nq, D), lambda i: (0, 0)),
                  pl.BlockSpec(keys2.shape, lambda i: (0, 0, 0))],
        out_specs=[pl.BlockSpec((D, tm), lambda i: (0, i)), key_spec, key_spec, key_spec, key_spec],
        out_shape=[jax.ShapeDtypeStruct((D, T), BF16), per_key(F32), per_key(F32), per_key(BF16), per_key(BF16)],
        scratch_shapes=[pltpu.VMEM((2 * PEER_HEADS, PEER_NKEYS, tm), F32), rank_rows, rank_rows,
                        pltpu.VMEM((PEER_TOPK, PEER_HEADS, tm), F32), pltpu.VMEM((PEER_HEADS, tm), F32)],
        compiler_params=_cparams(("parallel",)),
        name="peer_route",
    )(x2, g.reshape(1, D), wqt, keys2)


PEER_ABLK = 8


def _gelu(x):
    return 0.5 * x * (1.0 + lax.erf(x * (2.0 ** -0.5)))


def _peer_dense_kernel(xnt_ref, u_ref, vt_ref, sel_ref, na_ref, e1_ref, rk_ref, e2_ref, x_ref, o_ref,
                       acc_ref, m_ref, rep_ref):
    j = pl.program_id(1)

    @pl.when(j == 0)
    def _():
        acc_ref[...] = jnp.zeros_like(acc_ref)

    tt = m_ref.shape[1]
    zero = jnp.zeros((BF16_ROWS, tt), BF16)
    npair = PEER_HEADS * PEER_ABLK
    rows = jnp.concatenate([na_ref[...].reshape(npair, tt), e1_ref[...].reshape(npair, tt)], axis=0).astype(BF16)
    rep_ref[...] = _dot(sel_ref[...], rows).astype(BF16)
    for al in range(PEER_ABLK):
        for g in range(PEER_NKEYS // BF16_ROWS):
            bs = slice(g * BF16_ROWS, (g + 1) * BF16_ROWS)
            w = None
            for h in range(PEER_HEADS):
                r_n = (h * PEER_ABLK + al) * BF16_ROWS
                r_e = (npair + h * PEER_ABLK + al) * BF16_ROWS
                e2m = jnp.where(rk_ref[h, bs, :] < rep_ref[r_n:r_n + BF16_ROWS, :], e2_ref[h, bs, :], zero)
                wh = e2m * rep_ref[r_e:r_e + BF16_ROWS, :]
                w = wh if w is None else w + wh
            m_ref[al * PEER_NKEYS + g * BF16_ROWS:al * PEER_NKEYS + (g + 1) * BF16_ROWS, :] = w
    act = _gelu(_dot(u_ref[...], xnt_ref[...]).astype(BF16))
    m_ref[...] = m_ref[...] * act
    acc_ref[...] += _dot(vt_ref[...], m_ref[...])

    @pl.when(j == pl.num_programs(1) - 1)
    def _():
        o_ref[...] = x_ref[...] + acc_ref[...].T


def _peer_dense(x2, xnt, u, v, na, e1, rk, e2, tt):
    T, D = x2.shape
    ne = u.shape[0]
    eb = PEER_ABLK * PEER_NKEYS
    ub = u.astype(BF16)
    vt = v.astype(BF16).T
    nrep = 2 * PEER_HEADS * PEER_ABLK
    sel = (jnp.arange(nrep * BF16_ROWS)[:, None] // BF16_ROWS == jnp.arange(nrep)[None, :]).astype(BF16)
    key_a = pl.BlockSpec((PEER_HEADS, PEER_ABLK, tt), lambda i, j: (0, j, i))
    key_b = pl.BlockSpec((PEER_HEADS, PEER_NKEYS, tt), lambda i, j: (0, 0, i))
    return pl.pallas_call(
        _peer_dense_kernel,
        grid=(T // tt, ne // eb),
        in_specs=[pl.BlockSpec((D, tt), lambda i, j: (0, i)),
                  pl.BlockSpec((eb, D), lambda i, j: (j, 0)),
                  pl.BlockSpec((D, eb), lambda i, j: (0, j)),
                  pl.BlockSpec(sel.shape, lambda i, j: (0, 0)),
                  key_a, key_a, key_b, key_b,
                  pl.BlockSpec((tt, D), lambda i, j: (i, 0))],
        out_specs=pl.BlockSpec((tt, D), lambda i, j: (i, 0)),
        out_shape=jax.ShapeDtypeStruct((T, D), F32),
        scratch_shapes=[pltpu.VMEM((D, tt), F32), pltpu.VMEM((eb, tt), BF16),
                        pltpu.VMEM((nrep * BF16_ROWS, tt), BF16)],
        compiler_params=_cparams(("parallel", "arbitrary")),
        name="peer_dense",
    )(xnt, ub, vt, sel, na, e1, rk, e2, x2)


def _peer(x2, norm_g, wq, keys, u, v):
    T, D = x2.shape
    xnt, na, e1, rk, e2 = _peer_route(x2, norm_g, wq, keys, tm=512)
    return _peer_dense(x2, xnt, u, v, na, e1, rk, e2, tt=512)


def kernel(x, ev_norm_mix, ev_w_in, ev_conv_w, ev_conv_b, ev_conv_ln_g, ev_conv_ln_b, ev_gate_w2, ev_gate_b,
           ev_gla_norm_g, ev_w_out, od_norm_mix, od_w_in, od_fgate_b, od_q_norm_g, od_k_norm_g, od_w_out,
           ffn_norm, peer_wq, peer_keys, peer_u, peer_v):
    B, S, D = x.shape
    x2 = x.reshape(B * S, D)
    x2 = _even_mix(x2, B, S, ev_norm_mix[0], ev_w_in[0], ev_conv_w[0], ev_conv_b[0], ev_conv_ln_g[0],
                   ev_conv_ln_b[0], ev_gate_w2[0], ev_gate_b[0], ev_gla_norm_g[0], ev_w_out[0])
    x2 = _peer(x2, ffn_norm[0], peer_wq[0], peer_keys[0], peer_u[0], peer_v[0])
    x2 = _fox_mix(x2, B, S, od_norm_mix[0], od_w_in[0], od_fgate_b[0], od_q_norm_g[0], od_k_norm_g[0], od_w_out[0])
    x2 = _peer(x2, ffn_norm[1], peer_wq[1], peer_keys[1], peer_u[1], peer_v[1])
    return x2.reshape(B, S, D)
```

```python
import functools

import jax
import jax.numpy as jnp
from jax import lax
from jax.experimental import pallas as pl
from jax.experimental.pallas import tpu as pltpu

F32 = jnp.float32
BF16 = jnp.bfloat16
HIGHEST = lax.Precision.HIGHEST

EPS = 1e-6
D_MODEL = 1024
CHUNK = 64
CONV_CH = 512
CONV_WIDTH = 31
GLA_HEADS = 4
GLA_DK = 64
GLA_DV = 128
GLA_GATE_RANK = 16
GLA_GATE_TAU = 16.0
FOX_HEADS = 16
FOX_HD = 64
PEER_HEADS = 8
PEER_NKEYS = 128
PEER_TOPK = 16
PEER_HALF = 128

LANES = 128
BF16_ROWS = 16
VMEM_LIMIT = 56 * 1024 * 1024


def _cparams(sem):
    return pltpu.CompilerParams(dimension_semantics=sem, vmem_limit_bytes=VMEM_LIMIT)


def _rms(x, g):
    ms = jnp.mean(x * x, axis=-1, keepdims=True)
    return x * lax.rsqrt(ms + EPS) * g


def _log_sigmoid(z):
    return jnp.minimum(z, 0.0) - jnp.log(1.0 + jnp.exp(-jnp.abs(z)))


def _dot(a, b, **kw):
    return jnp.dot(a, b, preferred_element_type=F32, **kw)


def _dot_nt(a, b, **kw):
    return lax.dot_general(a, b, (((1,), (1,)), ((), ())), preferred_element_type=F32, **kw)


def _even_in_kernel(x_ref, g_ref, w_ref, wkt_ref, wgt_ref,
                    aval_ref, agate_ref, q_ref, v_ref, r_ref, glr_ref, kt_ref, glrt_ref):
    xn = _rms(x_ref[...], g_ref[...]).astype(BF16)
    off = 0
    for o_ref in (aval_ref, agate_ref, q_ref, v_ref, r_ref, glr_ref):
        n = o_ref.shape[-1]
        o_ref[...] = _dot(xn, w_ref[:, off:off + n]).astype(o_ref.dtype)
        off += n
    kt_ref[...] = _dot_nt(wkt_ref[...], xn)
    glrt_ref[...] = _dot_nt(wgt_ref[...], xn)


def _even_in(x2, g, w_in, tm):
    T, D = x2.shape
    hk = GLA_HEADS * GLA_DK
    hv = GLA_HEADS * GLA_DV
    c0, c1, c2, c3, c4, c5 = CONV_CH, 2 * CONV_CH, 2 * CONV_CH + hk, 2 * CONV_CH + 2 * hk, 2 * CONV_CH + 2 * hk + hv, 2 * CONV_CH + 2 * hk + 2 * hv
    wb = w_in.astype(BF16)
    w_tok = jnp.concatenate([wb[:, :c2], wb[:, c3:]], axis=1)
    wkt = wb[:, c2:c3].T
    wgt = wb[:, c5:].T
    widths = (CONV_CH, CONV_CH, hk, hv, hv, GLA_GATE_RANK)
    out_shape = [jax.ShapeDtypeStruct((T, n), F32) for n in widths]
    out_shape += [jax.ShapeDtypeStruct((hk, T), F32), jax.ShapeDtypeStruct((GLA_GATE_RANK, T), F32)]
    out_specs = [pl.BlockSpec((tm, n), lambda i: (i, 0)) for n in widths]
    out_specs += [pl.BlockSpec((hk, tm), lambda i: (0, i)), pl.BlockSpec((GLA_GATE_RANK, tm), lambda i: (0, i))]
    return pl.pallas_call(
        _even_in_kernel,
        grid=(T // tm,),
        in_specs=[
            pl.BlockSpec((tm, D), lambda i: (i, 0)),
            pl.BlockSpec((1, D), lambda i: (0, 0)),
            pl.BlockSpec(w_tok.shape, lambda i: (0, 0)),
            pl.BlockSpec(wkt.shape, lambda i: (0, 0)),
            pl.BlockSpec(wgt.shape, lambda i: (0, 0)),
        ],
        out_specs=out_specs,
        out_shape=out_shape,
        compiler_params=_cparams(("parallel",)),
        name="even_in",
    )(x2, g.reshape(1, D), w_tok, wkt, wgt)


CONV_HALO = 32
CONV_ROWS = 64


def _conv_kernel(val_ref, gate_ref, hval_ref, hgate_ref, w_ref, b_ref, lg_ref, lb_ref, o_ref, ext_ref, *, ts):
    first = pl.program_id(1) == 0
    u_halo = hval_ref[0] * jax.nn.sigmoid(hgate_ref[0])
    ext_ref[0:CONV_HALO, :] = jnp.where(first, 0.0, u_halo)
    ext_ref[CONV_HALO:CONV_HALO + ts, :] = val_ref[0] * jax.nn.sigmoid(gate_ref[0])
    lead = CONV_HALO - (CONV_WIDTH - 1)
    for r0 in range(0, ts, CONV_ROWS):
        acc = jnp.broadcast_to(b_ref[...], (CONV_ROWS, CONV_CH))
        for k in range(CONV_WIDTH):
            acc = acc + w_ref[k:k + 1, :] * ext_ref[r0 + lead + k:r0 + lead + k + CONV_ROWS, :]
        mu = jnp.mean(acc, axis=-1, keepdims=True)
        xc = acc - mu
        y = xc * lax.rsqrt(jnp.mean(xc * xc, axis=-1, keepdims=True) + EPS) * lg_ref[...] + lb_ref[...]
        o_ref[0, r0:r0 + CONV_ROWS, :] = (y * jax.nn.sigmoid(y)).astype(o_ref.dtype)


def _conformer_conv(val, gate, conv_w, conv_b, ln_g, ln_b, ts):
    B, S, C = val.shape
    hb = ts // CONV_HALO
    cur = pl.BlockSpec((1, ts, C), lambda b, i: (b, i, 0))
    halo = pl.BlockSpec((1, CONV_HALO, C), lambda b, i: (b, jnp.maximum(i * hb - 1, 0), 0))
    vec = pl.BlockSpec((1, C), lambda b, i: (0, 0))
    return pl.pallas_call(
        functools.partial(_conv_kernel, ts=ts),
        grid=(B, S // ts),
        in_specs=[cur, cur, halo, halo, pl.BlockSpec((CONV_WIDTH, C), lambda b, i: (0, 0)), vec, vec, vec],
        out_specs=pl.BlockSpec((1, ts, C), lambda b, i: (b, i, 0)),
        out_shape=jax.ShapeDtypeStruct((B, S, C), BF16),
        scratch_shapes=[pltpu.VMEM((CONV_HALO + ts, C), F32)],
        compiler_params=_cparams(("parallel", "parallel")),
        name="conformer_conv",
    )(val, gate, val, gate, conv_w, conv_b.reshape(1, C), ln_g.reshape(1, C), ln_b.reshape(1, C))


def _gla_kernel(q_ref, v_ref, r_ref, glr_ref, kt_ref, glrt_ref, w2_ref, w2t_ref, gb_ref, gbt_ref, ng_ref,
                o_ref, state_ref, *, ts):
    @pl.when(pl.program_id(1) == 0)
    def _():
        state_ref[...] = jnp.zeros_like(state_ref)

    L = CHUNK
    row = lax.broadcasted_iota(jnp.int32, (L, L), 0)
    col = lax.broadcasted_iota(jnp.int32, (L, L), 1)
    causal = row >= col
    tril = causal.astype(F32)
    triu = (row <= col).astype(F32)
    mid = L // 2 - 1
    scale = GLA_DK ** -0.5

    la = _log_sigmoid(_dot(glr_ref[0], w2_ref[...], precision=HIGHEST) + gb_ref[...]) * (1.0 / GLA_GATE_TAU)
    lat = _log_sigmoid(_dot(w2t_ref[...], glrt_ref[...], precision=HIGHEST) + gbt_ref[...]) * (1.0 / GLA_GATE_TAU)

    for c in range(ts // L):
        sl = slice(c * L, (c + 1) * L)
        b = _dot(tril, la[sl, :], precision=HIGHEST)
        bt = _dot(lat[:, sl], triu, precision=HIGHEST)
        qc = q_ref[0, sl, :] * scale
        q_in = (qc * jnp.exp(b)).astype(BF16)
        q_a = (qc * jnp.exp(b - b[mid:mid + 1, :])).astype(BF16)
        kt = kt_ref[:, sl]
        bt_last = bt[:, L - 1:L]
        k_a = (kt * jnp.exp(bt[:, mid:mid + 1] - bt)).astype(BF16)
        k_dec = (kt * jnp.exp(bt_last - bt)).astype(BF16)
        v_c = v_ref[0, sl, :].astype(BF16)
        r_c = r_ref[0, sl, :]
        for h in range(GLA_HEADS):
            ks = slice(h * GLA_DK, (h + 1) * GLA_DK)
            vs = slice(h * GLA_DV, (h + 1) * GLA_DV)
            state = state_ref[h]
            attn = jnp.where(causal, _dot(q_a[:, ks], k_a[ks, :]), 0.0)
            o = _dot(q_in[:, ks], state.astype(BF16)) + _dot(attn.astype(BF16), v_c[:, vs])
            state_ref[h] = state * jnp.exp(bt_last[ks, :]) + _dot(k_dec[ks, :], v_c[:, vs])
            y = _rms(o, ng_ref[...])
            rg = r_c[:, vs]
            o_ref[0, sl, vs] = (y * (rg * jax.nn.sigmoid(rg))).astype(o_ref.dtype)


def _gla(q, v, r, glr, kt, glrt, gate_w2, gate_b, norm_g, B, S, ts):
    hk = GLA_HEADS * GLA_DK
    hv = GLA_HEADS * GLA_DV
    nt = S // ts
    tok = lambda n: pl.BlockSpec((1, ts, n), lambda b, i: (b, i, 0))
    tr = lambda n: pl.BlockSpec((n, ts), lambda b, i: (0, b * nt + i))
    full = lambda shp: pl.BlockSpec(shp, lambda b, i: (0,) * len(shp))
    return pl.pallas_call(
        functools.partial(_gla_kernel, ts=ts),
        grid=(B, nt),
        in_specs=[tok(hk), tok(hv), tok(hv), tok(GLA_GATE_RANK), tr(hk), tr(GLA_GATE_RANK),
                  full((GLA_GATE_RANK, hk)), full((hk, GLA_GATE_RANK)), full((1, hk)), full((hk, 1)), full((1, GLA_DV))],
        out_specs=tok(hv),
        out_shape=jax.ShapeDtypeStruct((B, S, hv), BF16),
        scratch_shapes=[pltpu.VMEM((GLA_HEADS, GLA_DK, GLA_DV), F32)],
        compiler_params=_cparams(("parallel", "arbitrary")),
        name="gla",
    )(q.reshape(B, S, hk), v.reshape(B, S, hv), r.reshape(B, S, hv), glr.reshape(B, S, GLA_GATE_RANK), kt, glrt,
      gate_w2, gate_w2.T, gate_b.reshape(1, hk), gate_b.reshape(hk, 1), norm_g.reshape(1, GLA_DV))


def _out2_kernel(x_ref, a_ref, b_ref, wa_ref, wb_ref, o_ref):
    o_ref[...] = x_ref[...] + _dot(a_ref[...], wa_ref[...]) + _dot(b_ref[...], wb_ref[...])


def _out_proj2(x2, a, b, w_out, tm):
    T, D = x2.shape
    ka, kb = a.shape[1], b.shape[1]
    wb16 = w_out.astype(BF16)
    return pl.pallas_call(
        _out2_kernel,
        grid=(T // tm,),
        in_specs=[pl.BlockSpec((tm, D), lambda i: (i, 0)), pl.BlockSpec((tm, ka), lambda i: (i, 0)),
                  pl.BlockSpec((tm, kb), lambda i: (i, 0)), pl.BlockSpec((ka, D), lambda i: (0, 0)),
                  pl.BlockSpec((kb, D), lambda i: (0, 0))],
        out_specs=pl.BlockSpec((tm, D), lambda i: (i, 0)),
        out_shape=jax.ShapeDtypeStruct((T, D), F32),
        compiler_params=_cparams(("parallel",)),
        name="out_proj2",
    )(x2, a, b, wb16[:ka], wb16[ka:])


def _out1_kernel(x_ref, a_ref, w_ref, o_ref):
    o_ref[...] = x_ref[...] + _dot(a_ref[...], w_ref[...])


def _out_proj1(x2, a, w_out, tm):
    T, D = x2.shape
    ka = a.shape[1]
    return pl.pallas_call(
        _out1_kernel,
        grid=(T // tm,),
        in_specs=[pl.BlockSpec((tm, D), lambda i: (i, 0)), pl.BlockSpec((tm, ka), lambda i: (i, 0)),
                  pl.BlockSpec((ka, D), lambda i: (0, 0))],
        out_specs=pl.BlockSpec((tm, D), lambda i: (i, 0)),
        out_shape=jax.ShapeDtypeStruct((T, D), F32),
        compiler_params=_cparams(("parallel",)),
        name="out_proj1",
    )(x2, a, w_out.astype(BF16))


def _even_mix(x2, B, S, norm_g, w_in, conv_w, conv_b, ln_g, ln_b, gate_w2, gate_b, gla_norm_g, w_out):
    a_val, a_gate, q, v, r, glr, kt, glrt = _even_in(x2, norm_g, w_in, tm=512)
    y_a = _conformer_conv(a_val.reshape(B, S, CONV_CH), a_gate.reshape(B, S, CONV_CH), conv_w, conv_b, ln_g, ln_b, ts=512)
    y_b = _gla(q, v, r, glr, kt, glrt, gate_w2, gate_b, gla_norm_g, B, S, ts=512)
    return _out_proj2(x2, y_a.reshape(B * S, CONV_CH), y_b.reshape(B * S, GLA_HEADS * GLA_DV), w_out, tm=512)


def _fox_in_kernel(x_ref, g_ref, w_ref, wft_ref, wf_ref, fb_ref, fbr_ref, qg_ref, kg_ref, hm_ref,
                   q_ref, k_ref, v_ref, c_ref, ct_ref, carry_ref, carry_tok_ref, *, ts):
    @pl.when(pl.program_id(1) == 0)
    def _():
        carry_ref[...] = jnp.zeros_like(carry_ref)
        carry_tok_ref[...] = jnp.zeros_like(carry_tok_ref)

    xn = _rms(x_ref[0], g_ref[...]).astype(BF16)
    hd = FOX_HEADS * FOX_HD

    def head_rms(t, gain):
        ms = _dot((t * t).astype(BF16), hm_ref[...])
        return (t * lax.rsqrt(ms + EPS) * gain)

    q_ref[0] = head_rms(_dot(xn, w_ref[:, 0:hd]), qg_ref[...]).astype(q_ref.dtype)
    k_ref[0] = head_rms(_dot(xn, w_ref[:, hd:2 * hd]), kg_ref[...]).astype(k_ref.dtype)
    v_ref[0] = _dot(xn, w_ref[:, 2 * hd:3 * hd]).astype(v_ref.dtype)
    lf = _log_sigmoid(_dot_nt(wft_ref[...], xn) + fb_ref[...])
    row = lax.broadcasted_iota(jnp.int32, (ts, ts), 0)
    col = lax.broadcasted_iota(jnp.int32, (ts, ts), 1)
    c = _dot(lf, (row <= col).astype(F32), precision=HIGHEST) + carry_ref[...]
    c_ref[0] = c
    carry_ref[...] = c[:, ts - 1:ts]
    lf_tok = _log_sigmoid(_dot(xn, wf_ref[...]) + fbr_ref[...])
    c_tok = _dot((row >= col).astype(F32), lf_tok, precision=HIGHEST) + carry_tok_ref[...]
    ct_ref[0] = c_tok
    carry_tok_ref[...] = c_tok[ts - 1:ts, :]


def _fox_in(x2, B, S, g, w_in, fgate_b, q_g, k_g, ts):
    D = x2.shape[1]
    hd = FOX_HEADS * FOX_HD
    wb = w_in.astype(BF16)
    w_tok = wb[:, :3 * hd]
    wf = wb[:, 3 * hd:]
    wft = wf.T
    head_shape = jax.ShapeDtypeStruct((B, S, hd), BF16)
    head_spec = pl.BlockSpec((1, ts, hd), lambda b, i: (b, i, 0))
    full = lambda shp: pl.BlockSpec(shp, lambda b, i: (0,) * len(shp))
    head_of = jnp.arange(hd) // FOX_HD
    head_mean = jnp.where(head_of[:, None] == head_of[None, :], 1.0 / FOX_HD, 0.0).astype(BF16)
    q_gain = jnp.tile(q_g * FOX_HD ** -0.5, FOX_HEADS).reshape(1, hd)
    k_gain = jnp.tile(k_g, FOX_HEADS).reshape(1, hd)
    return pl.pallas_call(
        functools.partial(_fox_in_kernel, ts=ts),
        grid=(B, S // ts),
        in_specs=[pl.BlockSpec((1, ts, D), lambda b, i: (b, i, 0)), full((1, D)), full(w_tok.shape), full(wft.shape),
                  full(wf.shape), full((FOX_HEADS, 1)), full((1, FOX_HEADS)), full((1, hd)), full((1, hd)),
                  full((hd, hd))],
        out_specs=[head_spec, head_spec, head_spec, pl.BlockSpec((1, FOX_HEADS, ts), lambda b, i: (b, 0, i)),
                   pl.BlockSpec((1, ts, FOX_HEADS), lambda b, i: (b, i, 0))],
        out_shape=[head_shape, head_shape, head_shape, jax.ShapeDtypeStruct((B, FOX_HEADS, S), F32),
                   jax.ShapeDtypeStruct((B, S, FOX_HEADS), F32)],
        scratch_shapes=[pltpu.VMEM((FOX_HEADS, 1), F32), pltpu.VMEM((1, FOX_HEADS), F32)],
        compiler_params=_cparams(("parallel", "arbitrary")),
        name="fox_in",
    )(x2.reshape(B, S, D), g.reshape(1, D), w_tok, wft, wf, fgate_b.reshape(FOX_HEADS, 1),
      fgate_b.reshape(1, FOX_HEADS), q_gain, k_gain, head_mean)


FOX_PAIR = 2


FOX_EXP_ZERO = 110.0


FOX_DIRECT_BOUND = 75.0


def _fox_attn_kernel(bound_ref, q_ref, kin_ref, vin_ref, c_ref, ct_ref, o_ref, k_ref, v_ref, *, tq):
    qi = pl.program_id(2)

    @pl.when(qi == 0)
    def _():
        for hh in range(FOX_PAIR):
            hs = slice(hh * FOX_HD, (hh + 1) * FOX_HD)
            k_ref[hh] = kin_ref[0, :, hs]
            v_ref[hh] = vin_ref[0, :, hs]

    row = lax.broadcasted_iota(jnp.int32, (tq, tq), 0)
    col = lax.broadcasted_iota(jnp.int32, (tq, tq), 1)
    causal = row >= col
    nt = c_ref.shape[3]
    tile_id = lax.broadcasted_iota(jnp.int32, (nt, 1), 0)
    head_lane = lax.broadcasted_iota(jnp.int32, (tq, FOX_HEADS), 1)

    def head_out(hh, direct):
        q = q_ref[0, :, hh * FOX_HD:(hh + 1) * FOX_HD]
        c_q = jnp.sum(jnp.where(tile_id == qi, c_ref[0, 0, hh, :, 0:1], 0.0), axis=0, keepdims=True)
        c_k = c_ref[0, 0, hh, :, tq - 1:tq]
        dead = (tile_id < qi) & (2.0 * bound_ref[0, 0] + c_q - c_k < -FOX_EXP_ZERO)
        lo = jnp.sum(dead.astype(jnp.int32))

        def step(ki, carry, masked):
            m, l, acc = carry
            k0 = pl.multiple_of(ki * tq, tq)
            k = k_ref[hh, pl.ds(k0, tq), :]
            v = v_ref[hh, pl.ds(k0, tq), :]
            s = _dot_nt(q, k) - c_ref[0, 0, hh, pl.ds(ki, 1), :]
            if masked:
                s = jnp.where(causal, s, -jnp.inf)
            m_new = jnp.maximum(m, jnp.max(s, axis=-1, keepdims=True))
            p = jnp.exp(s - m_new)
            alpha = jnp.exp(m - m_new)
            l = alpha * l + jnp.sum(p, axis=-1, keepdims=True)
            acc = alpha * acc + _dot(p.astype(BF16), v)
            return m_new, l, acc

        if not direct:
            init = (jnp.full((tq, 1), -jnp.inf, F32), jnp.zeros((tq, 1), F32), jnp.zeros((tq, FOX_HD), F32))
            carry = lax.fori_loop(lo, qi, functools.partial(step, masked=False), init)
            _, l, acc = step(qi, carry, True)
            return acc / l

        head = pl.program_id(1) * FOX_PAIR + hh
        c_col = jnp.sum(jnp.where(head_lane == head, ct_ref[0], 0.0), axis=-1, keepdims=True)

        def dstep(ki, carry, masked):
            l, acc = carry
            k0 = pl.multiple_of(ki * tq, tq)
            s = _dot_nt(q, k_ref[hh, pl.ds(k0, tq), :]) - c_ref[0, 0, hh, pl.ds(ki, 1), :] + c_col
            if masked:
                s = jnp.where(causal, s, -jnp.inf)
            p = jnp.exp(s)
            l = l + jnp.sum(p, axis=-1, keepdims=True)
            acc = acc + _dot(p.astype(BF16), v_ref[hh, pl.ds(k0, tq), :])
            return l, acc

        init = (jnp.zeros((tq, 1), F32), jnp.zeros((tq, FOX_HD), F32))
        carry = lax.fori_loop(lo, qi, functools.partial(dstep, masked=False), init)
        l, acc = dstep(qi, carry, True)
        return acc / l

    direct_ok = bound_ref[0, 0] < FOX_DIRECT_BOUND

    @pl.when(direct_ok)
    def _():
        o_ref[0] = jnp.concatenate([head_out(hh, True) for hh in range(FOX_PAIR)], axis=-1).astype(o_ref.dtype)

    @pl.when(jnp.logical_not(direct_ok))
    def _():
        o_ref[0] = jnp.concatenate([head_out(hh, False) for hh in range(FOX_PAIR)], axis=-1).astype(o_ref.dtype)


def _fox_attn(q, k, v, c, c_tok, bound, B, S, tq):
    npair = FOX_HEADS // FOX_PAIR
    c5 = c.reshape(B, npair, FOX_PAIR, S // tq, tq)
    pw = FOX_PAIR * FOX_HD
    qspec = pl.BlockSpec((1, tq, pw), lambda b, h, i: (b, i, h))
    kvspec = pl.BlockSpec((1, S, pw), lambda b, h, i: (b, 0, h))
    return pl.pallas_call(
        functools.partial(_fox_attn_kernel, tq=tq),
        grid=(B, npair, S // tq),
        in_specs=[pl.BlockSpec(memory_space=pltpu.SMEM), qspec, kvspec, kvspec,
                  pl.BlockSpec((1, 1, FOX_PAIR, S // tq, tq), lambda b, h, i: (b, h, 0, 0, 0)),
                  pl.BlockSpec((1, tq, FOX_HEADS), lambda b, h, i: (b, i, 0))],
        out_specs=pl.BlockSpec((1, tq, FOX_PAIR * FOX_HD), lambda b, h, i: (b, i, h)),
        out_shape=jax.ShapeDtypeStruct((B, S, FOX_HEADS * FOX_HD), BF16),
        scratch_shapes=[pltpu.VMEM((FOX_PAIR, S, FOX_HD), BF16), pltpu.VMEM((FOX_PAIR, S, FOX_HD), BF16)],
        compiler_params=_cparams(("parallel", "parallel", "arbitrary")),
        name="fox_attn",
    )(bound, q, k, v, c5, c_tok)


FOX_NORM_SLACK = 1.025


def _fox_mix(x2, B, S, norm_g, w_in, fgate_b, q_g, k_g, w_out):
    q, k, v, c, c_tok = _fox_in(x2, B, S, norm_g, w_in, fgate_b, q_g, k_g, ts=512)
    bound = (FOX_NORM_SLACK * FOX_HD ** 0.5 * jnp.max(jnp.abs(q_g)) * jnp.max(jnp.abs(k_g))).reshape(1, 1).astype(F32)
    o = _fox_attn(q, k, v, c, c_tok, bound, B, S, tq=512)
    return _out_proj1(x2, o.reshape(B * S, FOX_HEADS * FOX_HD), w_out, tm=512)


_PEER_CELLS = tuple((r, c) for r in range(PEER_TOPK) for c in range(PEER_TOPK) if (r + 1) * (c + 1) <= PEER_TOPK)


def _peer_route_kernel(x_ref, g_ref, wqt_ref, keys_ref, xnt_ref, na_ref, e1_ref, rk_ref, e2_ref,
                       s_ref, val_ref, idx_ref, nrow_ref, invz_ref):
    tm = x_ref.shape[0]
    xn = _rms(x_ref[...], g_ref[...])
    xnt_ref[...] = xn.T.astype(xnt_ref.dtype)
    qt = _dot_nt(wqt_ref[...], xn.astype(BF16)).astype(BF16)
    for j in range(2 * PEER_HEADS):
        s_ref[j] = _dot(keys_ref[j], qt[j * PEER_HALF:(j + 1) * PEER_HALF, :])
    kio = lax.broadcasted_iota(jnp.int32, (PEER_NKEYS, tm), 0).astype(F32)
    nkeys = float(PEER_NKEYS)

    def extract(j, carry):
        h = j // 2
        half = j % 2
        work = s_ref[j]
        for r in range(PEER_TOPK):
            m = jnp.max(work, axis=0, keepdims=True)
            idx = jnp.min(jnp.where(work == m, kio, nkeys), axis=0, keepdims=True)
            val_ref[half, r, pl.ds(h, 1), :] = m
            idx_ref[half, r, pl.ds(h, 1), :] = idx
            if r + 1 < PEER_TOPK:
                work = jnp.where(kio == idx, -jnp.inf, work)
        return carry

    lax.fori_loop(0, 2 * PEER_HEADS, extract, 0)

    v1 = [val_ref[0, r] for r in range(PEER_TOPK)]
    v2 = [val_ref[1, c] for c in range(PEER_TOPK)]
    cand = [v1[r] + v2[c] for r, c in _PEER_CELLS]
    flat = [r * PEER_TOPK + c for r, c in _PEER_CELLS]
    big = PEER_TOPK * PEER_TOPK
    sel_val, sel_row = [], []
    for j in range(PEER_TOPK):
        m = functools.reduce(jnp.maximum, cand)
        idx = functools.reduce(jnp.minimum, [jnp.where(cv == m, f, big) for cv, f in zip(cand, flat)])
        sel_val.append(m)
        sel_row.append(idx >> 4)
        if j + 1 < PEER_TOPK:
            cand = [jnp.where(idx == f, -jnp.inf, cv) for cv, f in zip(cand, flat)]
    z = functools.reduce(jnp.add, [jnp.exp(m - sel_val[0]) for m in sel_val])
    invz_ref[...] = 1.0 / z
    for r in range(PEER_TOPK):
        nrow_ref[r] = functools.reduce(jnp.add, [jnp.where(sr == r, 1.0, 0.0) for sr in sel_row])

    def emit(h, carry):
        row = pl.ds(h, 1)
        na = jnp.zeros((PEER_NKEYS, tm), F32)
        for r in range(PEER_TOPK):
            na = jnp.where(kio == idx_ref[0, r, row, :], nrow_ref[r, row, :], na)
        rk = jnp.full((PEER_NKEYS, tm), float(PEER_TOPK), F32)
        for c in range(PEER_TOPK):
            rk = jnp.where(kio == idx_ref[1, c, row, :], float(c), rk)
        na_ref[h] = na
        rk_ref[h] = rk.astype(rk_ref.dtype)
        e1_ref[h] = jnp.exp(s_ref[2 * h] - val_ref[0, 0, row, :]) * invz_ref[row, :]
        e2_ref[h] = jnp.exp(s_ref[2 * h + 1] - val_ref[1, 0, row, :]).astype(e2_ref.dtype)
        return carry

    lax.fori_loop(0, PEER_HEADS, emit, 0)


def _peer_route(x2, g, wq, keys, tm):
    T, D = x2.shape
    nq = wq.shape[1]
    wqt = wq.astype(BF16).T
    keys2 = keys.reshape(2 * PEER_HEADS, PEER_NKEYS, PEER_HALF).astype(BF16)
    per_key = lambda dt: jax.ShapeDtypeStruct((PEER_HEADS, PEER_NKEYS, T), dt)
    key_spec = pl.BlockSpec((PEER_HEADS, PEER_NKEYS, tm), lambda i: (0, 0, i))
    rank_rows = pltpu.VMEM((2, PEER_TOPK, PEER_HEADS, tm), F32)
    return pl.pallas_call(
        _peer_route_kernel,
        grid=(T // tm,),
        in_specs=[pl.BlockSpec((tm, D), lambda i: (i, 0)), pl.BlockSpec((1, D), lambda i: (0, 0)),
                  pl.BlockSpec((nq, D), lambda i: (0, 0)),
                  pl.BlockSpec(keys2.shape, lambda i: (0, 0, 0))],
        out_specs=[pl.BlockSpec((D, tm), lambda i: (0, i)), key_spec, key_spec, key_spec, key_spec],
        out_shape=[jax.ShapeDtypeStruct((D, T), BF16), per_key(F32), per_key(F32), per_key(BF16), per_key(BF16)],
        scratch_shapes=[pltpu.VMEM((2 * PEER_HEADS, PEER_NKEYS, tm), F32), rank_rows, rank_rows,
                        pltpu.VMEM((PEER_TOPK, PEER_HEADS, tm), F32), pltpu.VMEM((PEER_HEADS, tm), F32)],
        compiler_params=_cparams(("parallel",)),
        name="peer_route",
    )(x2, g.reshape(1, D), wqt, keys2)


PEER_ABLK = 8


def _gelu(x):
    return 0.5 * x * (1.0 + lax.erf(x * (2.0 ** -0.5)))


def _peer_dense_kernel(xnt_ref, u_ref, vt_ref, sel_ref, na_ref, e1_ref, rk_ref, e2_ref, x_ref, o_ref,
                       acc_ref, m_ref, rep_ref):
    j = pl.program_id(1)

    @pl.when(j == 0)
    def _():
        acc_ref[...] = jnp.zeros_like(acc_ref)

    tt = m_ref.shape[1]
    zero = jnp.zeros((BF16_ROWS, tt), BF16)
    npair = PEER_HEADS * PEER_ABLK
    rows = jnp.concatenate([na_ref[...].reshape(npair, tt), e1_ref[...].reshape(npair, tt)], axis=0).astype(BF16)
    rep_ref[...] = _dot(sel_ref[...], rows).astype(BF16)
    for al in range(PEER_ABLK):
        for g in range(PEER_NKEYS // BF16_ROWS):
            bs = slice(g * BF16_ROWS, (g + 1) * BF16_ROWS)
            w = None
            for h in range(PEER_HEADS):
                r_n = (h * PEER_ABLK + al) * BF16_ROWS
                r_e = (npair + h * PEER_ABLK + al) * BF16_ROWS
                e2m = jnp.where(rk_ref[h, bs, :] < rep_ref[r_n:r_n + BF16_ROWS, :], e2_ref[h, bs, :], zero)
                wh = e2m * rep_ref[r_e:r_e + BF16_ROWS, :]
                w = wh if w is None else w + wh
            m_ref[al * PEER_NKEYS + g * BF16_ROWS:al * PEER_NKEYS + (g + 1) * BF16_ROWS, :] = w
    act = _gelu(_dot(u_ref[...], xnt_ref[...]).astype(BF16))
    m_ref[...] = m_ref[...] * act
    acc_ref[...] += _dot(vt_ref[...], m_ref[...])

    @pl.when(j == pl.num_programs(1) - 1)
    def _():
        o_ref[...] = x_ref[...] + acc_ref[...].T


def _peer_dense(x2, xnt, u, v, na, e1, rk, e2, tt):
    T, D = x2.shape
    ne = u.shape[0]
    eb = PEER_ABLK * PEER_NKEYS
    ub = u.astype(BF16)
    vt = v.astype(BF16).T
    nrep = 2 * PEER_HEADS * PEER_ABLK
    sel = (jnp.arange(nrep * BF16_ROWS)[:, None] // BF16_ROWS == jnp.arange(nrep)[None, :]).astype(BF16)
    key_a = pl.BlockSpec((PEER_HEADS, PEER_ABLK, tt), lambda i, j: (0, j, i))
    key_b = pl.BlockSpec((PEER_HEADS, PEER_NKEYS, tt), lambda i, j: (0, 0, i))
    return pl.pallas_call(
        _peer_dense_kernel,
        grid=(T // tt, ne // eb),
        in_specs=[pl.BlockSpec((D, tt), lambda i, j: (0, i)),
                  pl.BlockSpec((eb, D), lambda i, j: (j, 0)),
                  pl.BlockSpec((D, eb), lambda i, j: (0, j)),
                  pl.BlockSpec(sel.shape, lambda i, j: (0, 0)),
                  key_a, key_a, key_b, key_b,
                  pl.BlockSpec((tt, D), lambda i, j: (i, 0))],
        out_specs=pl.BlockSpec((tt, D), lambda i, j: (i, 0)),
        out_shape=jax.ShapeDtypeStruct((T, D), F32),
        scratch_shapes=[pltpu.VMEM((D, tt), F32), pltpu.VMEM((eb, tt), BF16),
                        pltpu.VMEM((nrep * BF16_ROWS, tt), BF16)],
        compiler_params=_cparams(("parallel", "arbitrary")),
        name="peer_dense",
    )(xnt, ub, vt, sel, na, e1, rk, e2, x2)


def _peer(x2, norm_g, wq, keys, u, v):
    T, D = x2.shape
    xnt, na, e1, rk, e2 = _peer_route(x2, norm_g, wq, keys, tm=512)
    return _peer_dense(x2, xnt, u, v, na, e1, rk, e2, tt=512)


def kernel(x, ev_norm_mix, ev_w_in, ev_conv_w, ev_conv_b, ev_conv_ln_g, ev_conv_ln_b, ev_gate_w2, ev_gate_b,
           ev_gla_norm_g, ev_w_out, od_norm_mix, od_w_in, od_fgate_b, od_q_norm_g, od_k_norm_g, od_w_out,
           ffn_norm, peer_wq, peer_keys, peer_u, peer_v):
    B, S, D = x.shape
    x2 = x.reshape(B * S, D)
    x2 = _even_mix(x2, B, S, ev_norm_mix[0], ev_w_in[0], ev_conv_w[0], ev_conv_b[0], ev_conv_ln_g[0],
                   ev_conv_ln_b[0], ev_gate_w2[0], ev_gate_b[0], ev_gla_norm_g[0], ev_w_out[0])
    x2 = _peer(x2, ffn_norm[0], peer_wq[0], peer_keys[0], peer_u[0], peer_v[0])
    x2 = _fox_mix(x2, B, S, od_norm_mix[0], od_w_in[0], od_fgate_b[0], od_q_norm_g[0], od_k_norm_g[0], od_w_out[0])
    x2 = _peer(x2, ffn_norm[1], peer_wq[1], peer_keys[1], peer_u[1], peer_v[1])
    return x2.reshape(B, S, D)
```

```python
import functools

import jax
import jax.numpy as jnp
from jax import lax
from jax.experimental import pallas as pl
from jax.experimental.pallas import tpu as pltpu

F32 = jnp.float32
BF16 = jnp.bfloat16
HIGHEST = lax.Precision.HIGHEST

EPS = 1e-6
D_MODEL = 1024
CHUNK = 64
CONV_CH = 512
CONV_WIDTH = 31
GLA_HEADS = 4
GLA_DK = 64
GLA_DV = 128
GLA_GATE_RANK = 16
GLA_GATE_TAU = 16.0
FOX_HEADS = 16
FOX_HD = 64
PEER_HEADS = 8
PEER_NKEYS = 128
PEER_TOPK = 16
PEER_HALF = 128

LANES = 128
SUBLANES = 8
BF16_ROWS = 16
VMEM_LIMIT = 56 * 1024 * 1024


def _cparams(sem):
    return pltpu.CompilerParams(dimension_semantics=sem, vmem_limit_bytes=VMEM_LIMIT)


def _rms(x, g):
    ms = jnp.mean(x * x, axis=-1, keepdims=True)
    return x * lax.rsqrt(ms + EPS) * g


def _log_sigmoid(z):
    return jnp.minimum(z, 0.0) - jnp.log(1.0 + jnp.exp(-jnp.abs(z)))


def _split3(x):
    hi = x.astype(BF16)
    r1 = x - hi.astype(F32)
    mid = r1.astype(BF16)
    lo = (r1 - mid.astype(F32)).astype(BF16)
    return hi, mid, lo


def _dot(a, b, **kw):
    return jnp.dot(a, b, preferred_element_type=F32, **kw)


def _dot_nt(a, b, **kw):
    return lax.dot_general(a, b, (((1,), (1,)), ((), ())), preferred_element_type=F32, **kw)


def _even_in_kernel(x_ref, g_ref, w_ref, wkt_ref, wgt_ref,
                    aval_ref, agate_ref, q_ref, v_ref, r_ref, glr_ref, kt_ref, glrt_ref):
    xn = _rms(x_ref[...], g_ref[...]).astype(BF16)
    off = 0
    for o_ref in (aval_ref, agate_ref, q_ref, v_ref, r_ref, glr_ref):
        n = o_ref.shape[-1]
        o_ref[...] = _dot(xn, w_ref[:, off:off + n]).astype(o_ref.dtype)
        off += n
    kt_ref[...] = _dot_nt(wkt_ref[...], xn)
    glrt_ref[...] = _dot_nt(wgt_ref[...], xn)


def _even_in(x2, g, w_in, tm):
    T, D = x2.shape
    hk = GLA_HEADS * GLA_DK
    hv = GLA_HEADS * GLA_DV
    c0, c1, c2, c3, c4, c5 = CONV_CH, 2 * CONV_CH, 2 * CONV_CH + hk, 2 * CONV_CH + 2 * hk, 2 * CONV_CH + 2 * hk + hv, 2 * CONV_CH + 2 * hk + 2 * hv
    wb = w_in.astype(BF16)
    w_tok = jnp.concatenate([wb[:, :c2], wb[:, c3:]], axis=1)
    wkt = wb[:, c2:c3].T
    wgt = wb[:, c5:].T
    widths = (CONV_CH, CONV_CH, hk, hv, hv, GLA_GATE_RANK)
    out_shape = [jax.ShapeDtypeStruct((T, n), F32) for n in widths]
    out_shape += [jax.ShapeDtypeStruct((hk, T), F32), jax.ShapeDtypeStruct((GLA_GATE_RANK, T), F32)]
    out_specs = [pl.BlockSpec((tm, n), lambda i: (i, 0)) for n in widths]
    out_specs += [pl.BlockSpec((hk, tm), lambda i: (0, i)), pl.BlockSpec((GLA_GATE_RANK, tm), lambda i: (0, i))]
    return pl.pallas_call(
        _even_in_kernel,
        grid=(T // tm,),
        in_specs=[
            pl.BlockSpec((tm, D), lambda i: (i, 0)),
            pl.BlockSpec((1, D), lambda i: (0, 0)),
            pl.BlockSpec(w_tok.shape, lambda i: (0, 0)),
            pl.BlockSpec(wkt.shape, lambda i: (0, 0)),
            pl.BlockSpec(wgt.shape, lambda i: (0, 0)),
        ],
        out_specs=out_specs,
        out_shape=out_shape,
        compiler_params=_cparams(("parallel",)),
        name="even_in",
    )(x2, g.reshape(1, D), w_tok, wkt, wgt)


CONV_HALO = 32
CONV_ROWS = 64


def _conv_kernel(val_ref, gate_ref, hval_ref, hgate_ref, w_ref, b_ref, lg_ref, lb_ref, o_ref, ext_ref, al_ref, *, ts):
    first = pl.program_id(1) == 0
    u_halo = hval_ref[0] * jax.nn.sigmoid(hgate_ref[0])
    ext_ref[0:CONV_HALO, :] = jnp.where(first, 0.0, u_halo)
    ext_ref[CONV_HALO:CONV_HALO + ts, :] = val_ref[0] * jax.nn.sigmoid(gate_ref[0])
    lead = CONV_HALO - (CONV_WIDTH - 1)
    for r0 in range(0, ts, CONV_ROWS):
        acc = jnp.broadcast_to(b_ref[...], (CONV_ROWS, CONV_CH))
        for a in range(SUBLANES):
            taps = [k for k in range(CONV_WIDTH) if (lead + k) % SUBLANES == a]
            if not taps:
                continue
            span = max((lead + k) // SUBLANES for k in taps) * SUBLANES + CONV_ROWS
            al_ref[0:span, :] = ext_ref[r0 + a:r0 + a + span, :]
            for k in taps:
                m0 = (lead + k) // SUBLANES * SUBLANES
                acc = acc + w_ref[k:k + 1, :] * al_ref[m0:m0 + CONV_ROWS, :]
        mu = jnp.mean(acc, axis=-1, keepdims=True)
        xc = acc - mu
        y = xc * lax.rsqrt(jnp.mean(xc * xc, axis=-1, keepdims=True) + EPS) * lg_ref[...] + lb_ref[...]
        o_ref[0, r0:r0 + CONV_ROWS, :] = (y * jax.nn.sigmoid(y)).astype(o_ref.dtype)


def _conformer_conv(val, gate, conv_w, conv_b, ln_g, ln_b, ts):
    B, S, C = val.shape
    hb = ts // CONV_HALO
    cur = pl.BlockSpec((1, ts, C), lambda b, i: (b, i, 0))
    halo = pl.BlockSpec((1, CONV_HALO, C), lambda b, i: (b, jnp.maximum(i * hb - 1, 0), 0))
    vec = pl.BlockSpec((1, C), lambda b, i: (0, 0))
    return pl.pallas_call(
        functools.partial(_conv_kernel, ts=ts),
        grid=(B, S // ts),
        in_specs=[cur, cur, halo, halo, pl.BlockSpec((CONV_WIDTH, C), lambda b, i: (0, 0)), vec, vec, vec],
        out_specs=pl.BlockSpec((1, ts, C), lambda b, i: (b, i, 0)),
        out_shape=jax.ShapeDtypeStruct((B, S, C), BF16),
        scratch_shapes=[pltpu.VMEM((CONV_HALO + ts, C), F32), pltpu.VMEM((CONV_HALO + CONV_ROWS, C), F32)],
        compiler_params=_cparams(("parallel", "parallel")),
        name="conformer_conv",
    )(val, gate, val, gate, conv_w, conv_b.reshape(1, C), ln_g.reshape(1, C), ln_b.reshape(1, C))


def _gla_kernel(q_ref, v_ref, r_ref, glr_ref, kt_ref, glrt_ref, w2_ref, w2t_ref, gb_ref, gbt_ref, ng_ref,
                o_ref, state_ref, *, ts):
    @pl.when(pl.program_id(1) == 0)
    def _():
        state_ref[...] = jnp.zeros_like(state_ref)

    L = CHUNK
    row = lax.broadcasted_iota(jnp.int32, (L, L), 0)
    col = lax.broadcasted_iota(jnp.int32, (L, L), 1)
    causal = row >= col
    tril = jnp.where(causal, 1.0, 0.0).astype(BF16)
    triu = jnp.where(row <= col, 1.0, 0.0).astype(BF16)
    mid = L // 2 - 1
    scale = GLA_DK ** -0.5

    la = _log_sigmoid(_dot(glr_ref[0], w2_ref[...], precision=HIGHEST) + gb_ref[...]) * (1.0 / GLA_GATE_TAU)
    lat = _log_sigmoid(_dot(w2t_ref[...], glrt_ref[...], precision=HIGHEST) + gbt_ref[...]) * (1.0 / GLA_GATE_TAU)

    for c in range(ts // L):
        sl = slice(c * L, (c + 1) * L)
        hk = GLA_HEADS * GLA_DK
        b3 = _dot(tril, jnp.concatenate(_split3(la[sl, :]), axis=1))
        b = b3[:, 0:hk] + b3[:, hk:2 * hk] + b3[:, 2 * hk:3 * hk]
        bt3 = _dot(jnp.concatenate(_split3(lat[:, sl]), axis=0), triu)
        bt = bt3[0:hk] + bt3[hk:2 * hk] + bt3[2 * hk:3 * hk]
        qc = q_ref[0, sl, :] * scale
        q_in = (qc * jnp.exp(b)).astype(BF16)
        q_a = (qc * jnp.exp(b - b[mid:mid + 1, :])).astype(BF16)
        kt = kt_ref[:, sl]
        bt_last = bt[:, L - 1:L]
        k_a = (kt * jnp.exp(bt[:, mid:mid + 1] - bt)).astype(BF16)
        k_dec = (kt * jnp.exp(bt_last - bt)).astype(BF16)
        v_c = v_ref[0, sl, :].astype(BF16)
        r_c = r_ref[0, sl, :]
        for h in range(GLA_HEADS):
            ks = slice(h * GLA_DK, (h + 1) * GLA_DK)
            vs = slice(h * GLA_DV, (h + 1) * GLA_DV)
            state = state_ref[h]
            attn = jnp.where(causal, _dot(q_a[:, ks], k_a[ks, :]), 0.0)
            o = _dot(q_in[:, ks], state.astype(BF16)) + _dot(attn.astype(BF16), v_c[:, vs])
            state_ref[h] = state * jnp.exp(bt_last[ks, :]) + _dot(k_dec[ks, :], v_c[:, vs])
            y = _rms(o, ng_ref[...])
            rg = r_c[:, vs]
            o_ref[0, sl, vs] = (y * (rg * jax.nn.sigmoid(rg))).astype(o_ref.dtype)


def _gla(q, v, r, glr, kt, glrt, gate_w2, gate_b, norm_g, B, S, ts):
    hk = GLA_HEADS * GLA_DK
    hv = GLA_HEADS * GLA_DV
    nt = S // ts
    tok = lambda n: pl.BlockSpec((1, ts, n), lambda b, i: (b, i, 0))
    tr = lambda n: pl.BlockSpec((n, ts), lambda b, i: (0, b * nt + i))
    full = lambda shp: pl.BlockSpec(shp, lambda b, i: (0,) * len(shp))
    return pl.pallas_call(
        functools.partial(_gla_kernel, ts=ts),
        grid=(B, nt),
        in_specs=[tok(hk), tok(hv), tok(hv), tok(GLA_GATE_RANK), tr(hk), tr(GLA_GATE_RANK),
                  full((GLA_GATE_RANK, hk)), full((hk, GLA_GATE_RANK)), full((1, hk)), full((hk, 1)), full((1, GLA_DV))],
        out_specs=tok(hv),
        out_shape=jax.ShapeDtypeStruct((B, S, hv), BF16),
        scratch_shapes=[pltpu.VMEM((GLA_HEADS, GLA_DK, GLA_DV), F32)],
        compiler_params=_cparams(("parallel", "arbitrary")),
        name="gla",
    )(q.reshape(B, S, hk), v.reshape(B, S, hv), r.reshape(B, S, hv), glr.reshape(B, S, GLA_GATE_RANK), kt, glrt,
      gate_w2, gate_w2.T, gate_b.reshape(1, hk), gate_b.reshape(hk, 1), norm_g.reshape(1, GLA_DV))


def _out2_kernel(x_ref, a_ref, b_ref, wa_ref, wb_ref, o_ref):
    o_ref[...] = x_ref[...] + _dot(a_ref[...], wa_ref[...]) + _dot(b_ref[...], wb_ref[...])


def _out_proj2(x2, a, b, w_out, tm):
    T, D = x2.shape
    ka, kb = a.shape[1], b.shape[1]
    wb16 = w_out.astype(BF16)
    return pl.pallas_call(
        _out2_kernel,
        grid=(T // tm,),
        in_specs=[pl.BlockSpec((tm, D), lambda i: (i, 0)), pl.BlockSpec((tm, ka), lambda i: (i, 0)),
                  pl.BlockSpec((tm, kb), lambda i: (i, 0)), pl.BlockSpec((ka, D), lambda i: (0, 0)),
                  pl.BlockSpec((kb, D), lambda i: (0, 0))],
        out_specs=pl.BlockSpec((tm, D), lambda i: (i, 0)),
        out_shape=jax.ShapeDtypeStruct((T, D), F32),
        compiler_params=_cparams(("parallel",)),
        name="out_proj2",
    )(x2, a, b, wb16[:ka], wb16[ka:])


def _out1_kernel(x_ref, a_ref, w_ref, o_ref):
    o_ref[...] = x_ref[...] + _dot(a_ref[...], w_ref[...])


def _out_proj1(x2, a, w_out, tm):
    T, D = x2.shape
    ka = a.shape[1]
    return pl.pallas_call(
        _out1_kernel,
        grid=(T // tm,),
        in_specs=[pl.BlockSpec((tm, D), lambda i: (i, 0)), pl.BlockSpec((tm, ka), lambda i: (i, 0)),
                  pl.BlockSpec((ka, D), lambda i: (0, 0))],
        out_specs=pl.BlockSpec((tm, D), lambda i: (i, 0)),
        out_shape=jax.ShapeDtypeStruct((T, D), F32),
        compiler_params=_cparams(("parallel",)),
        name="out_proj1",
    )(x2, a, w_out.astype(BF16))


def _even_mix(x2, B, S, norm_g, w_in, conv_w, conv_b, ln_g, ln_b, gate_w2, gate_b, gla_norm_g, w_out):
    a_val, a_gate, q, v, r, glr, kt, glrt = _even_in(x2, norm_g, w_in, tm=512)
    y_a = _conformer_conv(a_val.reshape(B, S, CONV_CH), a_gate.reshape(B, S, CONV_CH), conv_w, conv_b, ln_g, ln_b, ts=512)
    y_b = _gla(q, v, r, glr, kt, glrt, gate_w2, gate_b, gla_norm_g, B, S, ts=512)
    return _out_proj2(x2, y_a.reshape(B * S, CONV_CH), y_b.reshape(B * S, GLA_HEADS * GLA_DV), w_out, tm=512)


def _fox_in_kernel(x_ref, g_ref, w_ref, wft_ref, wf_ref, fb_ref, fbr_ref, qg_ref, kg_ref, hm_ref,
                   q_ref, k_ref, v_ref, c_ref, ct_ref, carry_ref, carry_tok_ref, *, ts):
    @pl.when(pl.program_id(1) == 0)
    def _():
        carry_ref[...] = jnp.zeros_like(carry_ref)
        carry_tok_ref[...] = jnp.zeros_like(carry_tok_ref)

    xn = _rms(x_ref[0], g_ref[...]).astype(BF16)
    hd = FOX_HEADS * FOX_HD

    def head_rms(t, gain):
        ms = _dot((t * t).astype(BF16), hm_ref[...])
        return (t * lax.rsqrt(ms + EPS) * gain)

    q_ref[0] = head_rms(_dot(xn, w_ref[:, 0:hd]), qg_ref[...]).astype(q_ref.dtype)
    k_ref[0] = head_rms(_dot(xn, w_ref[:, hd:2 * hd]), kg_ref[...]).astype(k_ref.dtype)
    v_ref[0] = _dot(xn, w_ref[:, 2 * hd:3 * hd]).astype(v_ref.dtype)
    lf = _log_sigmoid(_dot_nt(wft_ref[...], xn) + fb_ref[...])
    row = lax.broadcasted_iota(jnp.int32, (ts, ts), 0)
    col = lax.broadcasted_iota(jnp.int32, (ts, ts), 1)
    nh = FOX_HEADS
    parts = jnp.concatenate(_split3(lf), axis=0)
    cs = _dot(parts, jnp.where(row <= col, 1.0, 0.0).astype(BF16))
    c = cs[0:nh] + cs[nh:2 * nh] + cs[2 * nh:3 * nh] + carry_ref[...]
    c_ref[0] = c
    carry_ref[...] = c[:, ts - 1:ts]
    lf_tok = _log_sigmoid(_dot(xn, wf_ref[...]) + fbr_ref[...])
    parts_t = jnp.concatenate(_split3(lf_tok), axis=1)
    cst = _dot(jnp.where(row >= col, 1.0, 0.0).astype(BF16), parts_t)
    c_tok = cst[:, 0:nh] + cst[:, nh:2 * nh] + cst[:, 2 * nh:3 * nh] + carry_tok_ref[...]
    ct_ref[0] = c_tok
    carry_tok_ref[...] = c_tok[ts - 1:ts, :]


def _fox_in(x2, B, S, g, w_in, fgate_b, q_g, k_g, ts):
    D = x2.shape[1]
    hd = FOX_HEADS * FOX_HD
    wb = w_in.astype(BF16)
    w_tok = wb[:, :3 * hd]
    wf = wb[:, 3 * hd:]
    wft = wf.T
    head_shape = jax.ShapeDtypeStruct((B, S, hd), BF16)
    head_spec = pl.BlockSpec((1, ts, hd), lambda b, i: (b, i, 0))
    full = lambda shp: pl.BlockSpec(shp, lambda b, i: (0,) * len(shp))
    head_of = jnp.arange(hd) // FOX_HD
    head_mean = jnp.where(head_of[:, None] == head_of[None, :], 1.0 / FOX_HD, 0.0).astype(BF16)
    q_gain = jnp.tile(q_g * FOX_HD ** -0.5, FOX_HEADS).reshape(1, hd)
    k_gain = jnp.tile(k_g, FOX_HEADS).reshape(1, hd)
    return pl.pallas_call(
        functools.partial(_fox_in_kernel, ts=ts),
        grid=(B, S // ts),
        in_specs=[pl.BlockSpec((1, ts, D), lambda b, i: (b, i, 0)), full((1, D)), full(w_tok.shape), full(wft.shape),
                  full(wf.shape), full((FOX_HEADS, 1)), full((1, FOX_HEADS)), full((1, hd)), full((1, hd)),
                  full((hd, hd))],
        out_specs=[head_spec, head_spec, head_spec, pl.BlockSpec((1, FOX_HEADS, ts), lambda b, i: (b, 0, i)),
                   pl.BlockSpec((1, ts, FOX_HEADS), lambda b, i: (b, i, 0))],
        out_shape=[head_shape, head_shape, head_shape, jax.ShapeDtypeStruct((B, FOX_HEADS, S), F32),
                   jax.ShapeDtypeStruct((B, S, FOX_HEADS), F32)],
        scratch_shapes=[pltpu.VMEM((FOX_HEADS, 1), F32), pltpu.VMEM((1, FOX_HEADS), F32)],
        compiler_params=_cparams(("parallel", "arbitrary")),
        name="fox_in",
    )(x2.reshape(B, S, D), g.reshape(1, D), w_tok, wft, wf, fgate_b.reshape(FOX_HEADS, 1),
      fgate_b.reshape(1, FOX_HEADS), q_gain, k_gain, head_mean)


FOX_PAIR = 2


FOX_EXP_ZERO = 110.0


FOX_DIRECT_BOUND = 75.0


def _fox_attn_kernel(bound_ref, q_ref, kin_ref, vin_ref, c_ref, ct_ref, o_ref, k_ref, v_ref, *, tq):
    qi = pl.program_id(2)

    @pl.when(qi == 0)
    def _():
        for hh in range(FOX_PAIR):
            hs = slice(hh * FOX_HD, (hh + 1) * FOX_HD)
            k_ref[hh] = kin_ref[0, :, hs]
            v_ref[hh] = vin_ref[0, :, hs]

    row = lax.broadcasted_iota(jnp.int32, (tq, tq), 0)
    col = lax.broadcasted_iota(jnp.int32, (tq, tq), 1)
    causal = row >= col
    nt = c_ref.shape[3]
    tile_id = lax.broadcasted_iota(jnp.int32, (nt, 1), 0)
    head_lane = lax.broadcasted_iota(jnp.int32, (tq, FOX_HEADS), 1)

    def head_out(hh, direct):
        q = q_ref[0, :, hh * FOX_HD:(hh + 1) * FOX_HD]
        c_q = jnp.sum(jnp.where(tile_id == qi, c_ref[0, 0, hh, :, 0:1], 0.0), axis=0, keepdims=True)
        c_k = c_ref[0, 0, hh, :, tq - 1:tq]
        dead = (tile_id < qi) & (2.0 * bound_ref[0, 0] + c_q - c_k < -FOX_EXP_ZERO)
        lo = jnp.sum(dead.astype(jnp.int32))

        def step(ki, carry, masked):
            m, l, acc = carry
            k0 = pl.multiple_of(ki * tq, tq)
            k = k_ref[hh, pl.ds(k0, tq), :]
            v = v_ref[hh, pl.ds(k0, tq), :]
            s = _dot_nt(q, k) - c_ref[0, 0, hh, pl.ds(ki, 1), :]
            if masked:
                s = jnp.where(causal, s, -jnp.inf)
            m_new = jnp.maximum(m, jnp.max(s, axis=-1, keepdims=True))
            p = jnp.exp(s - m_new)
            alpha = jnp.exp(m - m_new)
            l = alpha * l + jnp.sum(p, axis=-1, keepdims=True)
            acc = alpha * acc + _dot(p.astype(BF16), v)
            return m_new, l, acc

        if not direct:
            init = (jnp.full((tq, 1), -jnp.inf, F32), jnp.zeros((tq, 1), F32), jnp.zeros((tq, FOX_HD), F32))
            carry = lax.fori_loop(lo, qi, functools.partial(step, masked=False), init)
            _, l, acc = step(qi, carry, True)
            return acc / l

        head = pl.program_id(1) * FOX_PAIR + hh
        c_col = jnp.sum(jnp.where(head_lane == head, ct_ref[0], 0.0), axis=-1, keepdims=True)

        def dstep(ki, carry, masked):
            l, acc = carry
            k0 = pl.multiple_of(ki * tq, tq)
            s = _dot_nt(q, k_ref[hh, pl.ds(k0, tq), :]) - c_ref[0, 0, hh, pl.ds(ki, 1), :] + c_col
            if masked:
                s = jnp.where(causal, s, -jnp.inf)
            p = jnp.exp(s)
            l = l + jnp.sum(p, axis=-1, keepdims=True)
            acc = acc + _dot(p.astype(BF16), v_ref[hh, pl.ds(k0, tq), :])
            return l, acc

        init = (jnp.zeros((tq, 1), F32), jnp.zeros((tq, FOX_HD), F32))
        carry = lax.fori_loop(lo, qi, functools.partial(dstep, masked=False), init)
        l, acc = dstep(qi, carry, True)
        return acc / l

    direct_ok = bound_ref[0, 0] < FOX_DIRECT_BOUND

    @pl.when(direct_ok)
    def _():
        o_ref[0] = jnp.concatenate([head_out(hh, True) for hh in range(FOX_PAIR)], axis=-1).astype(o_ref.dtype)

    @pl.when(jnp.logical_not(direct_ok))
    def _():
        o_ref[0] = jnp.concatenate([head_out(hh, False) for hh in range(FOX_PAIR)], axis=-1).astype(o_ref.dtype)


def _fox_attn(q, k, v, c, c_tok, bound, B, S, tq):
    npair = FOX_HEADS // FOX_PAIR
    c5 = c.reshape(B, npair, FOX_PAIR, S // tq, tq)
    pw = FOX_PAIR * FOX_HD
    qspec = pl.BlockSpec((1, tq, pw), lambda b, h, i: (b, i, h))
    kvspec = pl.BlockSpec((1, S, pw), lambda b, h, i: (b, 0, h))
    return pl.pallas_call(
        functools.partial(_fox_attn_kernel, tq=tq),
        grid=(B, npair, S // tq),
        in_specs=[pl.BlockSpec(memory_space=pltpu.SMEM), qspec, kvspec, kvspec,
                  pl.BlockSpec((1, 1, FOX_PAIR, S // tq, tq), lambda b, h, i: (b, h, 0, 0, 0)),
                  pl.BlockSpec((1, tq, FOX_HEADS), lambda b, h, i: (b, i, 0))],
        out_specs=pl.BlockSpec((1, tq, FOX_PAIR * FOX_HD), lambda b, h, i: (b, i, h)),
        out_shape=jax.ShapeDtypeStruct((B, S, FOX_HEADS * FOX_HD), BF16),
        scratch_shapes=[pltpu.VMEM((FOX_PAIR, S, FOX_HD), BF16), pltpu.VMEM((FOX_PAIR, S, FOX_HD), BF16)],
        compiler_params=_cparams(("parallel", "parallel", "arbitrary")),
        name="fox_attn",
    )(bound, q, k, v, c5, c_tok)


FOX_NORM_SLACK = 1.025


def _fox_mix(x2, B, S, norm_g, w_in, fgate_b, q_g, k_g, w_out):
    q, k, v, c, c_tok = _fox_in(x2, B, S, norm_g, w_in, fgate_b, q_g, k_g, ts=512)
    bound = (FOX_NORM_SLACK * FOX_HD ** 0.5 * jnp.max(jnp.abs(q_g)) * jnp.max(jnp.abs(k_g))).reshape(1, 1).astype(F32)
    o = _fox_attn(q, k, v, c, c_tok, bound, B, S, tq=512)
    return _out_proj1(x2, o.reshape(B * S, FOX_HEADS * FOX_HD), w_out, tm=512)


_PEER_CELLS = tuple((r, c) for r in range(PEER_TOPK) for c in range(PEER_TOPK) if (r + 1) * (c + 1) <= PEER_TOPK)


def _peer_route_kernel(x_ref, g_ref, wqt_ref, keys_ref, xnt_ref, na_ref, e1_ref, rk_ref, e2_ref,
                       s_ref, val_ref, idx_ref, nrow_ref, invz_ref):
    tm = x_ref.shape[0]
    xn = _rms(x_ref[...], g_ref[...])
    xnt_ref[...] = xn.T.astype(xnt_ref.dtype)
    qt = _dot_nt(wqt_ref[...], xn.astype(BF16)).astype(BF16)
    for j in range(2 * PEER_HEADS):
        s_ref[j] = _dot(keys_ref[j], qt[j * PEER_HALF:(j + 1) * PEER_HALF, :])
    kio = lax.broadcasted_iota(jnp.int32, (PEER_NKEYS, tm), 0).astype(F32)
    nkeys = float(PEER_NKEYS)

    def extract(j, carry):
        h = j // 2
        half = j % 2
        work = s_ref[j]
        for r in range(PEER_TOPK):
            m = jnp.max(work, axis=0, keepdims=True)
            idx = jnp.min(jnp.where(work == m, kio, nkeys), axis=0, keepdims=True)
            val_ref[half, r, pl.ds(h, 1), :] = m
            idx_ref[half, r, pl.ds(h, 1), :] = idx
            if r + 1 < PEER_TOPK:
                work = jnp.where(kio == idx, -jnp.inf, work)
        return carry

    lax.fori_loop(0, 2 * PEER_HEADS, extract, 0)

    v1 = [val_ref[0, r] for r in range(PEER_TOPK)]
    v2 = [val_ref[1, c] for c in range(PEER_TOPK)]
    cand = [v1[r] + v2[c] for r, c in _PEER_CELLS]
    flat = [r * PEER_TOPK + c for r, c in _PEER_CELLS]
    big = PEER_TOPK * PEER_TOPK
    sel_val, sel_row = [], []
    for j in range(PEER_TOPK):
        m = functools.reduce(jnp.maximum, cand)
        idx = functools.reduce(jnp.minimum, [jnp.where(cv == m, f, big) for cv, f in zip(cand, flat)])
        sel_val.append(m)
        sel_row.append(idx >> 4)
        if j + 1 < PEER_TOPK:
            cand = [jnp.where(idx == f, -jnp.inf, cv) for cv, f in zip(cand, flat)]
    z = functools.reduce(jnp.add, [jnp.exp(m - sel_val[0]) for m in sel_val])
    invz_ref[...] = 1.0 / z
    for r in range(PEER_TOPK):
        nrow_ref[r] = functools.reduce(jnp.add, [jnp.where(sr == r, 1.0, 0.0) for sr in sel_row])

    def emit(h, carry):
        row = pl.ds(h, 1)
        na = jnp.zeros((PEER_NKEYS, tm), F32)
        for r in range(PEER_TOPK):
            na = jnp.where(kio == idx_ref[0, r, row, :], nrow_ref[r, row, :], na)
        rk = jnp.full((PEER_NKEYS, tm), float(PEER_TOPK), F32)
        for c in range(PEER_TOPK):
            rk = jnp.where(kio == idx_ref[1, c, row, :], float(c), rk)
        na_ref[h] = na
        rk_ref[h] = rk.astype(rk_ref.dtype)
        e1_ref[h] = jnp.exp(s_ref[2 * h] - val_ref[0, 0, row, :]) * invz_ref[row, :]
        e2_ref[h] = jnp.exp(s_ref[2 * h + 1] - val_ref[1, 0, row, :]).astype(e2_ref.dtype)
        return carry

    lax.fori_loop(0, PEER_HEADS, emit, 0)


def _peer_route(x2, g, wq, keys, tm):
    T, D = x2.shape
    nq = wq.shape[1]
    wqt = wq.astype(BF16).T
    keys2 = keys.reshape(2 * PEER_HEADS, PEER_NKEYS, PEER_HALF).astype(BF16)
    per_key = lambda dt: jax.ShapeDtypeStruct((PEER_HEADS, PEER_NKEYS, T), dt)
    key_spec = pl.BlockSpec((PEER_HEADS, PEER_NKEYS, tm), lambda i: (0, 0, i))
    rank_rows = pltpu.VMEM((2, PEER_TOPK, PEER_HEADS, tm), F32)
    return pl.pallas_call(
        _peer_route_kernel,
        grid=(T // tm,),
        in_specs=[pl.BlockSpec((tm, D), lambda i: (i, 0)), pl.BlockSpec((1, D), lambda i: (0, 0)),
                  pl.BlockSpec((nq, D), lambda i: (0, 0)),
                  pl.BlockSpec(keys2.shape, lambda i: (0, 0, 0))],
        out_specs=[pl.BlockSpec((D, tm), lambda i: (0, i)), key_spec, key_spec, key_spec, key_spec],
        out_shape=[jax.ShapeDtypeStruct((D, T), BF16), per_key(F32), per_key(F32), per_key(BF16), per_key(BF16)],
        scratch_shapes=[pltpu.VMEM((2 * PEER_HEADS, PEER_NKEYS, tm), F32), rank_rows, rank_rows,
                        pltpu.VMEM((PEER_TOPK, PEER_HEADS, tm), F32), pltpu.VMEM((PEER_HEADS, tm), F32)],
        compiler_params=_cparams(("parallel",)),
        name="peer_route",
    )(x2, g.reshape(1, D), wqt, keys2)


PEER_ABLK = 8


def _gelu(x):
    return 0.5 * x * (1.0 + lax.erf(x * (2.0 ** -0.5)))


def _peer_dense_kernel(xnt_ref, u_ref, vt_ref, sel_ref, na_ref, e1_ref, rk_ref, e2_ref, x_ref, o_ref,
                       acc_ref, m_ref, rep_ref):
    j = pl.program_id(1)

    @pl.when(j == 0)
    def _():
        acc_ref[...] = jnp.zeros_like(acc_ref)

    tt = m_ref.shape[1]
    zero = jnp.zeros((BF16_ROWS, tt), BF16)
    npair = PEER_HEADS * PEER_ABLK
    rows = jnp.concatenate([na_ref[...].reshape(npair, tt), e1_ref[...].reshape(npair, tt)], axis=0).astype(BF16)
    rep_ref[...] = _dot(sel_ref[...], rows).astype(BF16)
    for al in range(PEER_ABLK):
        for g in range(PEER_NKEYS // BF16_ROWS):
            bs = slice(g * BF16_ROWS, (g + 1) * BF16_ROWS)
            w = None
            for h in range(PEER_HEADS):
                r_n = (h * PEER_ABLK + al) * BF16_ROWS
                r_e = (npair + h * PEER_ABLK + al) * BF16_ROWS
                e2m = jnp.where(rk_ref[h, bs, :] < rep_ref[r_n:r_n + BF16_ROWS, :], e2_ref[h, bs, :], zero)
                wh = e2m * rep_ref[r_e:r_e + BF16_ROWS, :]
                w = wh if w is None else w + wh
            m_ref[al * PEER_NKEYS + g * BF16_ROWS:al * PEER_NKEYS + (g + 1) * BF16_ROWS, :] = w
    act = _gelu(_dot(u_ref[...], xnt_ref[...]).astype(BF16))
    m_ref[...] = m_ref[...] * act
    acc_ref[...] += _dot(vt_ref[...], m_ref[...])

    @pl.when(j == pl.num_programs(1) - 1)
    def _():
        o_ref[...] = x_ref[...] + acc_ref[...].T


def _peer_dense(x2, xnt, u, v, na, e1, rk, e2, tt):
    T, D = x2.shape
    ne = u.shape[0]
    eb = PEER_ABLK * PEER_NKEYS
    ub = u.astype(BF16)
    vt = v.astype(BF16).T
    nrep = 2 * PEER_HEADS * PEER_ABLK
    sel = (jnp.arange(nrep * BF16_ROWS)[:, None] // BF16_ROWS == jnp.arange(nrep)[None, :]).astype(BF16)
    key_a = pl.BlockSpec((PEER_HEADS, PEER_ABLK, tt), lambda i, j: (0, j, i))
    key_b = pl.BlockSpec((PEER_HEADS, PEER_NKEYS, tt), lambda i, j: (0, 0, i))
    return pl.pallas_call(
        _peer_dense_kernel,
        grid=(T // tt, ne // eb),
        in_specs=[pl.BlockSpec((D, tt), lambda i, j: (0, i)),
                  pl.BlockSpec((eb, D), lambda i, j: (j, 0)),
                  pl.BlockSpec((D, eb), lambda i, j: (0, j)),
                  pl.BlockSpec(sel.shape, lambda i, j: (0, 0)),
                  key_a, key_a, key_b, key_b,
                  pl.BlockSpec((tt, D), lambda i, j: (i, 0))],
        out_specs=pl.BlockSpec((tt, D), lambda i, j: (i, 0)),
        out_shape=jax.ShapeDtypeStruct((T, D), F32),
        scratch_shapes=[pltpu.VMEM((D, tt), F32), pltpu.VMEM((eb, tt), BF16),
                        pltpu.VMEM((nrep * BF16_ROWS, tt), BF16)],
        compiler_params=_cparams(("parallel", "arbitrary")),
        name="peer_dense",
    )(xnt, ub, vt, sel, na, e1, rk, e2, x2)


def _peer(x2, norm_g, wq, keys, u, v):
    T, D = x2.shape
    xnt, na, e1, rk, e2 = _peer_route(x2, norm_g, wq, keys, tm=512)
    return _peer_dense(x2, xnt, u, v, na, e1, rk, e2, tt=512)


def kernel(x, ev_norm_mix, ev_w_in, ev_conv_w, ev_conv_b, ev_conv_ln_g, ev_conv_ln_b, ev_gate_w2, ev_gate_b,
           ev_gla_norm_g, ev_w_out, od_norm_mix, od_w_in, od_fgate_b, od_q_norm_g, od_k_norm_g, od_w_out,
           ffn_norm, peer_wq, peer_keys, peer_u, peer_v):
    B, S, D = x.shape
    x2 = x.reshape(B * S, D)
    x2 = _even_mix(x2, B, S, ev_norm_mix[0], ev_w_in[0], ev_conv_w[0], ev_conv_b[0], ev_conv_ln_g[0],
                   ev_conv_ln_b[0], ev_gate_w2[0], ev_gate_b[0], ev_gla_norm_g[0], ev_w_out[0])
    x2 = _peer(x2, ffn_norm[0], peer_wq[0], peer_keys[0], peer_u[0], peer_v[0])
    x2 = _fox_mix(x2, B, S, od_norm_mix[0], od_w_in[0], od_fgate_b[0], od_q_norm_g[0], od_k_norm_g[0], od_w_out[0])
    x2 = _peer(x2, ffn_norm[1], peer_wq[1], peer_keys[1], peer_u[1], peer_v[1])
    return x2.reshape(B, S, D)
```

```python
import functools

import jax
import jax.numpy as jnp
from jax import lax
from jax.experimental import pallas as pl
from jax.experimental.pallas import tpu as pltpu

F32 = jnp.float32
BF16 = jnp.bfloat16
HIGHEST = lax.Precision.HIGHEST

EPS = 1e-6
D_MODEL = 1024
CHUNK = 64
CONV_CH = 512
CONV_WIDTH = 31
GLA_HEADS = 4
GLA_DK = 64
GLA_DV = 128
GLA_GATE_RANK = 16
GLA_GATE_TAU = 16.0
FOX_HEADS = 16
FOX_HD = 64
PEER_HEADS = 8
PEER_NKEYS = 128
PEER_TOPK = 16
PEER_HALF = 128

LANES = 128
SUBLANES = 8
BF16_ROWS = 16
VMEM_LIMIT = 56 * 1024 * 1024


def _cparams(sem):
    return pltpu.CompilerParams(dimension_semantics=sem, vmem_limit_bytes=VMEM_LIMIT)


def _rms(x, g):
    ms = jnp.mean(x * x, axis=-1, keepdims=True)
    return x * lax.rsqrt(ms + EPS) * g


def _log_sigmoid(z):
    return jnp.minimum(z, 0.0) - jnp.log(1.0 + jnp.exp(-jnp.abs(z)))


def _split3(x):
    hi = x.astype(BF16)
    r1 = x - hi.astype(F32)
    mid = r1.astype(BF16)
    lo = (r1 - mid.astype(F32)).astype(BF16)
    return hi, mid, lo


def _dot(a, b, **kw):
    return jnp.dot(a, b, preferred_element_type=F32, **kw)


def _dot_nt(a, b, **kw):
    return lax.dot_general(a, b, (((1,), (1,)), ((), ())), preferred_element_type=F32, **kw)


def _even_in_kernel(x_ref, g_ref, w_ref, wkt_ref, wgt_ref,
                    aval_ref, agate_ref, q_ref, v_ref, r_ref, glr_ref, kt_ref, glrt_ref):
    xn = _rms(x_ref[...], g_ref[...]).astype(BF16)
    off = 0
    for o_ref in (aval_ref, agate_ref, q_ref, v_ref, r_ref, glr_ref):
        n = o_ref.shape[-1]
        o_ref[...] = _dot(xn, w_ref[:, off:off + n]).astype(o_ref.dtype)
        off += n
    kt_ref[...] = _dot_nt(wkt_ref[...], xn)
    glrt_ref[...] = _dot_nt(wgt_ref[...], xn)


def _even_in(x2, g, w_in, tm):
    T, D = x2.shape
    hk = GLA_HEADS * GLA_DK
    hv = GLA_HEADS * GLA_DV
    c0, c1, c2, c3, c4, c5 = CONV_CH, 2 * CONV_CH, 2 * CONV_CH + hk, 2 * CONV_CH + 2 * hk, 2 * CONV_CH + 2 * hk + hv, 2 * CONV_CH + 2 * hk + 2 * hv
    wb = w_in.astype(BF16)
    w_tok = jnp.concatenate([wb[:, :c2], wb[:, c3:]], axis=1)
    wkt = wb[:, c2:c3].T
    wgt = wb[:, c5:].T
    widths = (CONV_CH, CONV_CH, hk, hv, hv, GLA_GATE_RANK)
    out_shape = [jax.ShapeDtypeStruct((T, n), F32) for n in widths]
    out_shape += [jax.ShapeDtypeStruct((hk, T), F32), jax.ShapeDtypeStruct((GLA_GATE_RANK, T), F32)]
    out_specs = [pl.BlockSpec((tm, n), lambda i: (i, 0)) for n in widths]
    out_specs += [pl.BlockSpec((hk, tm), lambda i: (0, i)), pl.BlockSpec((GLA_GATE_RANK, tm), lambda i: (0, i))]
    return pl.pallas_call(
        _even_in_kernel,
        grid=(T // tm,),
        in_specs=[
            pl.BlockSpec((tm, D), lambda i: (i, 0)),
            pl.BlockSpec((1, D), lambda i: (0, 0)),
            pl.BlockSpec(w_tok.shape, lambda i: (0, 0)),
            pl.BlockSpec(wkt.shape, lambda i: (0, 0)),
            pl.BlockSpec(wgt.shape, lambda i: (0, 0)),
        ],
        out_specs=out_specs,
        out_shape=out_shape,
        compiler_params=_cparams(("parallel",)),
        name="even_in",
    )(x2, g.reshape(1, D), w_tok, wkt, wgt)


CONV_HALO = 32
CONV_ROWS = 64


def _conv_kernel(val_ref, gate_ref, hval_ref, hgate_ref, w_ref, b_ref, lg_ref, lb_ref, o_ref, ext_ref, al_ref, *, ts):
    first = pl.program_id(1) == 0
    u_halo = hval_ref[0] * jax.nn.sigmoid(hgate_ref[0])
    ext_ref[0:CONV_HALO, :] = jnp.where(first, 0.0, u_halo)
    ext_ref[CONV_HALO:CONV_HALO + ts, :] = val_ref[0] * jax.nn.sigmoid(gate_ref[0])
    lead = CONV_HALO - (CONV_WIDTH - 1)
    for r0 in range(0, ts, CONV_ROWS):
        acc = jnp.broadcast_to(b_ref[...], (CONV_ROWS, CONV_CH))
        for a in range(SUBLANES):
            taps = [k for k in range(CONV_WIDTH) if (lead + k) % SUBLANES == a]
            if not taps:
                continue
            span = max((lead + k) // SUBLANES for k in taps) * SUBLANES + CONV_ROWS
            al_ref[0:span, :] = ext_ref[r0 + a:r0 + a + span, :]
            for k in taps:
                m0 = (lead + k) // SUBLANES * SUBLANES
                acc = acc + w_ref[k:k + 1, :] * al_ref[m0:m0 + CONV_ROWS, :]
        mu = jnp.mean(acc, axis=-1, keepdims=True)
        xc = acc - mu
        y = xc * lax.rsqrt(jnp.mean(xc * xc, axis=-1, keepdims=True) + EPS) * lg_ref[...] + lb_ref[...]
        o_ref[0, r0:r0 + CONV_ROWS, :] = (y * jax.nn.sigmoid(y)).astype(o_ref.dtype)


def _conformer_conv(val, gate, conv_w, conv_b, ln_g, ln_b, ts):
    B, S, C = val.shape
    hb = ts // CONV_HALO
    cur = pl.BlockSpec((1, ts, C), lambda b, i: (b, i, 0))
    halo = pl.BlockSpec((1, CONV_HALO, C), lambda b, i: (b, jnp.maximum(i * hb - 1, 0), 0))
    vec = pl.BlockSpec((1, C), lambda b, i: (0, 0))
    return pl.pallas_call(
        functools.partial(_conv_kernel, ts=ts),
        grid=(B, S // ts),
        in_specs=[cur, cur, halo, halo, pl.BlockSpec((CONV_WIDTH, C), lambda b, i: (0, 0)), vec, vec, vec],
        out_specs=pl.BlockSpec((1, ts, C), lambda b, i: (b, i, 0)),
        out_shape=jax.ShapeDtypeStruct((B, S, C), BF16),
        scratch_shapes=[pltpu.VMEM((CONV_HALO + ts, C), F32), pltpu.VMEM((CONV_HALO + CONV_ROWS, C), F32)],
        compiler_params=_cparams(("parallel", "parallel")),
        name="conformer_conv",
    )(val, gate, val, gate, conv_w, conv_b.reshape(1, C), ln_g.reshape(1, C), ln_b.reshape(1, C))


def _gla_kernel(q_ref, v_ref, r_ref, glr_ref, kt_ref, glrt_ref, w2_ref, w2t_ref, gb_ref, gbt_ref, ng_ref,
                o_ref, state_ref, *, ts):
    @pl.when(pl.program_id(1) == 0)
    def _():
        state_ref[...] = jnp.zeros_like(state_ref)

    L = CHUNK
    row = lax.broadcasted_iota(jnp.int32, (L, L), 0)
    col = lax.broadcasted_iota(jnp.int32, (L, L), 1)
    causal = row >= col
    tril = jnp.where(causal, 1.0, 0.0).astype(BF16)
    triu = jnp.where(row <= col, 1.0, 0.0).astype(BF16)
    mid = L // 2 - 1
    scale = GLA_DK ** -0.5

    la = _log_sigmoid(_dot(glr_ref[0], w2_ref[...], precision=HIGHEST) + gb_ref[...]) * (1.0 / GLA_GATE_TAU)
    lat = _log_sigmoid(_dot(w2t_ref[...], glrt_ref[...], precision=HIGHEST) + gbt_ref[...]) * (1.0 / GLA_GATE_TAU)

    for c in range(ts // L):
        sl = slice(c * L, (c + 1) * L)
        hk = GLA_HEADS * GLA_DK
        b3 = _dot(tril, jnp.concatenate(_split3(la[sl, :]), axis=1))
        b = b3[:, 0:hk] + b3[:, hk:2 * hk] + b3[:, 2 * hk:3 * hk]
        bt3 = _dot(jnp.concatenate(_split3(lat[:, sl]), axis=0), triu)
        bt = bt3[0:hk] + bt3[hk:2 * hk] + bt3[2 * hk:3 * hk]
        qc = q_ref[0, sl, :] * scale
        q_in = (qc * jnp.exp(b)).astype(BF16)
        q_a = (qc * jnp.exp(b - b[mid:mid + 1, :])).astype(BF16)
        kt = kt_ref[:, sl]
        bt_last = bt[:, L - 1:L]
        k_a = (kt * jnp.exp(bt[:, mid:mid + 1] - bt)).astype(BF16)
        k_dec = (kt * jnp.exp(bt_last - bt)).astype(BF16)
        v_c = v_ref[0, sl, :].astype(BF16)
        r_c = r_ref[0, sl, :]
        for h in range(GLA_HEADS):
            ks = slice(h * GLA_DK, (h + 1) * GLA_DK)
            vs = slice(h * GLA_DV, (h + 1) * GLA_DV)
            state = state_ref[h]
            attn = jnp.where(causal, _dot(q_a[:, ks], k_a[ks, :]), 0.0)
            o = _dot(q_in[:, ks], state.astype(BF16)) + _dot(attn.astype(BF16), v_c[:, vs])
            state_ref[h] = state * jnp.exp(bt_last[ks, :]) + _dot(k_dec[ks, :], v_c[:, vs])
            y = _rms(o, ng_ref[...])
            rg = r_c[:, vs]
            o_ref[0, sl, vs] = (y * (rg * jax.nn.sigmoid(rg))).astype(o_ref.dtype)


def _gla(q, v, r, glr, kt, glrt, gate_w2, gate_b, norm_g, B, S, ts):
    hk = GLA_HEADS * GLA_DK
    hv = GLA_HEADS * GLA_DV
    nt = S // ts
    tok = lambda n: pl.BlockSpec((1, ts, n), lambda b, i: (b, i, 0))
    tr = lambda n: pl.BlockSpec((n, ts), lambda b, i: (0, b * nt + i))
    full = lambda shp: pl.BlockSpec(shp, lambda b, i: (0,) * len(shp))
    return pl.pallas_call(
        functools.partial(_gla_kernel, ts=ts),
        grid=(B, nt),
        in_specs=[tok(hk), tok(hv), tok(hv), tok(GLA_GATE_RANK), tr(hk), tr(GLA_GATE_RANK),
                  full((GLA_GATE_RANK, hk)), full((hk, GLA_GATE_RANK)), full((1, hk)), full((hk, 1)), full((1, GLA_DV))],
        out_specs=tok(hv),
        out_shape=jax.ShapeDtypeStruct((B, S, hv), BF16),
        scratch_shapes=[pltpu.VMEM((GLA_HEADS, GLA_DK, GLA_DV), F32)],
        compiler_params=_cparams(("parallel", "arbitrary")),
        name="gla",
    )(q.reshape(B, S, hk), v.reshape(B, S, hv), r.reshape(B, S, hv), glr.reshape(B, S, GLA_GATE_RANK), kt, glrt,
      gate_w2, gate_w2.T, gate_b.reshape(1, hk), gate_b.reshape(hk, 1), norm_g.reshape(1, GLA_DV))


def _out2_kernel(x_ref, a_ref, b_ref, wa_ref, wb_ref, o_ref):
    o_ref[...] = x_ref[...] + _dot(a_ref[...], wa_ref[...]) + _dot(b_ref[...], wb_ref[...])


def _out_proj2(x2, a, b, w_out, tm):
    T, D = x2.shape
    ka, kb = a.shape[1], b.shape[1]
    wb16 = w_out.astype(BF16)
    return pl.pallas_call(
        _out2_kernel,
        grid=(T // tm,),
        in_specs=[pl.BlockSpec((tm, D), lambda i: (i, 0)), pl.BlockSpec((tm, ka), lambda i: (i, 0)),
                  pl.BlockSpec((tm, kb), lambda i: (i, 0)), pl.BlockSpec((ka, D), lambda i: (0, 0)),
                  pl.BlockSpec((kb, D), lambda i: (0, 0))],
        out_specs=pl.BlockSpec((tm, D), lambda i: (i, 0)),
        out_shape=jax.ShapeDtypeStruct((T, D), F32),
        compiler_params=_cparams(("parallel",)),
        name="out_proj2",
    )(x2, a, b, wb16[:ka], wb16[ka:])


def _out1_kernel(x_ref, a_ref, w_ref, o_ref):
    o_ref[...] = x_ref[...] + _dot(a_ref[...], w_ref[...])


def _out_proj1(x2, a, w_out, tm):
    T, D = x2.shape
    ka = a.shape[1]
    return pl.pallas_call(
        _out1_kernel,
        grid=(T // tm,),
        in_specs=[pl.BlockSpec((tm, D), lambda i: (i, 0)), pl.BlockSpec((tm, ka), lambda i: (i, 0)),
                  pl.BlockSpec((ka, D), lambda i: (0, 0))],
        out_specs=pl.BlockSpec((tm, D), lambda i: (i, 0)),
        out_shape=jax.ShapeDtypeStruct((T, D), F32),
        compiler_params=_cparams(("parallel",)),
        name="out_proj1",
    )(x2, a, w_out.astype(BF16))


def _even_mix(x2, B, S, norm_g, w_in, conv_w, conv_b, ln_g, ln_b, gate_w2, gate_b, gla_norm_g, w_out):
    a_val, a_gate, q, v, r, glr, kt, glrt = _even_in(x2, norm_g, w_in, tm=512)
    y_a = _conformer_conv(a_val.reshape(B, S, CONV_CH), a_gate.reshape(B, S, CONV_CH), conv_w, conv_b, ln_g, ln_b, ts=512)
    y_b = _gla(q, v, r, glr, kt, glrt, gate_w2, gate_b, gla_norm_g, B, S, ts=512)
    return _out_proj2(x2, y_a.reshape(B * S, CONV_CH), y_b.reshape(B * S, GLA_HEADS * GLA_DV), w_out, tm=512)


def _fox_in_kernel(x_ref, g_ref, w_ref, wft_ref, wf_ref, fb_ref, fbr_ref, qg_ref, kg_ref, hm_ref,
                   q_ref, k_ref, v_ref, c_ref, ct_ref, carry_ref, carry_tok_ref, *, ts):
    @pl.when(pl.program_id(1) == 0)
    def _():
        carry_ref[...] = jnp.zeros_like(carry_ref)
        carry_tok_ref[...] = jnp.zeros_like(carry_tok_ref)

    xn = _rms(x_ref[0], g_ref[...]).astype(BF16)
    hd = FOX_HEADS * FOX_HD

    def head_rms(t, gain):
        ms = _dot((t * t).astype(BF16), hm_ref[...])
        return (t * lax.rsqrt(ms + EPS) * gain)

    q_ref[0] = head_rms(_dot(xn, w_ref[:, 0:hd]), qg_ref[...]).astype(q_ref.dtype)
    k_ref[0] = head_rms(_dot(xn, w_ref[:, hd:2 * hd]), kg_ref[...]).astype(k_ref.dtype)
    v_ref[0] = _dot(xn, w_ref[:, 2 * hd:3 * hd]).astype(v_ref.dtype)
    lf = _log_sigmoid(_dot_nt(wft_ref[...], xn) + fb_ref[...])
    row = lax.broadcasted_iota(jnp.int32, (ts, ts), 0)
    col = lax.broadcasted_iota(jnp.int32, (ts, ts), 1)
    nh = FOX_HEADS
    parts = jnp.concatenate(_split3(lf), axis=0)
    cs = _dot(parts, jnp.where(row <= col, 1.0, 0.0).astype(BF16))
    c = cs[0:nh] + cs[nh:2 * nh] + cs[2 * nh:3 * nh] + carry_ref[...]
    c_ref[0] = c
    carry_ref[...] = c[:, ts - 1:ts]
    lf_tok = _log_sigmoid(_dot(xn, wf_ref[...]) + fbr_ref[...])
    parts_t = jnp.concatenate(_split3(lf_tok), axis=1)
    cst = _dot(jnp.where(row >= col, 1.0, 0.0).astype(BF16), parts_t)
    c_tok = cst[:, 0:nh] + cst[:, nh:2 * nh] + cst[:, 2 * nh:3 * nh] + carry_tok_ref[...]
    ct_ref[0] = c_tok
    carry_tok_ref[...] = c_tok[ts - 1:ts, :]


def _fox_in(x2, B, S, g, w_in, fgate_b, q_g, k_g, ts):
    D = x2.shape[1]
    hd = FOX_HEADS * FOX_HD
    wb = w_in.astype(BF16)
    w_tok = wb[:, :3 * hd]
    wf = wb[:, 3 * hd:]
    wft = wf.T
    head_shape = jax.ShapeDtypeStruct((B, S, hd), BF16)
    head_spec = pl.BlockSpec((1, ts, hd), lambda b, i: (b, i, 0))
    full = lambda shp: pl.BlockSpec(shp, lambda b, i: (0,) * len(shp))
    head_of = jnp.arange(hd) // FOX_HD
    head_mean = jnp.where(head_of[:, None] == head_of[None, :], 1.0 / FOX_HD, 0.0).astype(BF16)
    q_gain = jnp.tile(q_g * FOX_HD ** -0.5, FOX_HEADS).reshape(1, hd)
    k_gain = jnp.tile(k_g, FOX_HEADS).reshape(1, hd)
    return pl.pallas_call(
        functools.partial(_fox_in_kernel, ts=ts),
        grid=(B, S // ts),
        in_specs=[pl.BlockSpec((1, ts, D), lambda b, i: (b, i, 0)), full((1, D)), full(w_tok.shape), full(wft.shape),
                  full(wf.shape), full((FOX_HEADS, 1)), full((1, FOX_HEADS)), full((1, hd)), full((1, hd)),
                  full((hd, hd))],
        out_specs=[head_spec, head_spec, head_spec, pl.BlockSpec((1, FOX_HEADS, ts), lambda b, i: (b, 0, i)),
                   pl.BlockSpec((1, ts, FOX_HEADS), lambda b, i: (b, i, 0))],
        out_shape=[head_shape, head_shape, head_shape, jax.ShapeDtypeStruct((B, FOX_HEADS, S), F32),
                   jax.ShapeDtypeStruct((B, S, FOX_HEADS), F32)],
        scratch_shapes=[pltpu.VMEM((FOX_HEADS, 1), F32), pltpu.VMEM((1, FOX_HEADS), F32)],
        compiler_params=_cparams(("parallel", "arbitrary")),
        name="fox_in",
    )(x2.reshape(B, S, D), g.reshape(1, D), w_tok, wft, wf, fgate_b.reshape(FOX_HEADS, 1),
      fgate_b.reshape(1, FOX_HEADS), q_gain, k_gain, head_mean)


FOX_PAIR = 2


FOX_EXP_ZERO = 110.0


FOX_DIRECT_BOUND = 75.0


def _fox_attn_kernel(bound_ref, q_ref, kin_ref, vin_ref, c_ref, ct_ref, o_ref, k_ref, v_ref, *, tq):
    qi = pl.program_id(2)

    @pl.when(qi == 0)
    def _():
        for hh in range(FOX_PAIR):
            hs = slice(hh * FOX_HD, (hh + 1) * FOX_HD)
            k_ref[hh] = kin_ref[0, :, hs]
            v_ref[hh] = vin_ref[0, :, hs]

    row = lax.broadcasted_iota(jnp.int32, (tq, tq), 0)
    col = lax.broadcasted_iota(jnp.int32, (tq, tq), 1)
    causal = row >= col
    nt = c_ref.shape[3]
    tile_id = lax.broadcasted_iota(jnp.int32, (nt, 1), 0)
    head_lane = lax.broadcasted_iota(jnp.int32, (tq, FOX_HEADS), 1)

    def head_out(hh, direct):
        q = q_ref[0, :, hh * FOX_HD:(hh + 1) * FOX_HD]
        c_q = jnp.sum(jnp.where(tile_id == qi, c_ref[0, 0, hh, :, 0:1], 0.0), axis=0, keepdims=True)
        c_k = c_ref[0, 0, hh, :, tq - 1:tq]
        dead = (tile_id < qi) & (2.0 * bound_ref[0, 0] + c_q - c_k < -FOX_EXP_ZERO)
        lo = jnp.sum(dead.astype(jnp.int32))

        def step(ki, carry, masked):
            m, l, acc = carry
            k0 = pl.multiple_of(ki * tq, tq)
            k = k_ref[hh, pl.ds(k0, tq), :]
            v = v_ref[hh, pl.ds(k0, tq), :]
            s = _dot_nt(q, k) - c_ref[0, 0, hh, pl.ds(ki, 1), :]
            if masked:
                s = jnp.where(causal, s, -jnp.inf)
            m_new = jnp.maximum(m, jnp.max(s, axis=-1, keepdims=True))
            p = jnp.exp(s - m_new)
            alpha = jnp.exp(m - m_new)
            l = alpha * l + jnp.sum(p, axis=-1, keepdims=True)
            acc = alpha * acc + _dot(p.astype(BF16), v)
            return m_new, l, acc

        if not direct:
            init = (jnp.full((tq, 1), -jnp.inf, F32), jnp.zeros((tq, 1), F32), jnp.zeros((tq, FOX_HD), F32))
            carry = lax.fori_loop(lo, qi, functools.partial(step, masked=False), init)
            _, l, acc = step(qi, carry, True)
            return acc / l

        head = pl.program_id(1) * FOX_PAIR + hh
        c_col = jnp.sum(jnp.where(head_lane == head, ct_ref[0], 0.0), axis=-1, keepdims=True)

        def dstep(ki, carry, masked):
            l, acc = carry
            k0 = pl.multiple_of(ki * tq, tq)
            s = _dot_nt(q, k_ref[hh, pl.ds(k0, tq), :]) - c_ref[0, 0, hh, pl.ds(ki, 1), :] + c_col
            if masked:
                s = jnp.where(causal, s, -jnp.inf)
            p = jnp.exp(s)
            l = l + jnp.sum(p, axis=-1, keepdims=True)
            acc = acc + _dot(p.astype(BF16), v_ref[hh, pl.ds(k0, tq), :])
            return l, acc

        init = (jnp.zeros((tq, 1), F32), jnp.zeros((tq, FOX_HD), F32))
        carry = lax.fori_loop(lo, qi, functools.partial(dstep, masked=False), init)
        l, acc = dstep(qi, carry, True)
        return acc / l

    direct_ok = bound_ref[0, 0] < FOX_DIRECT_BOUND

    @pl.when(direct_ok)
    def _():
        o_ref[0] = jnp.concatenate([head_out(hh, True) for hh in range(FOX_PAIR)], axis=-1).astype(o_ref.dtype)

    @pl.when(jnp.logical_not(direct_ok))
    def _():
        o_ref[0] = jnp.concatenate([head_out(hh, False) for hh in range(FOX_PAIR)], axis=-1).astype(o_ref.dtype)


def _fox_attn(q, k, v, c, c_tok, bound, B, S, tq):
    npair = FOX_HEADS // FOX_PAIR
    c5 = c.reshape(B, npair, FOX_PAIR, S // tq, tq)
    pw = FOX_PAIR * FOX_HD
    qspec = pl.BlockSpec((1, tq, pw), lambda b, h, i: (b, i, h))
    kvspec = pl.BlockSpec((1, S, pw), lambda b, h, i: (b, 0, h))
    return pl.pallas_call(
        functools.partial(_fox_attn_kernel, tq=tq),
        grid=(B, npair, S // tq),
        in_specs=[pl.BlockSpec(memory_space=pltpu.SMEM), qspec, kvspec, kvspec,
                  pl.BlockSpec((1, 1, FOX_PAIR, S // tq, tq), lambda b, h, i: (b, h, 0, 0, 0)),
                  pl.BlockSpec((1, tq, FOX_HEADS), lambda b, h, i: (b, i, 0))],
        out_specs=pl.BlockSpec((1, tq, FOX_PAIR * FOX_HD), lambda b, h, i: (b, i, h)),
        out_shape=jax.ShapeDtypeStruct((B, S, FOX_HEADS * FOX_HD), BF16),
        scratch_shapes=[pltpu.VMEM((FOX_PAIR, S, FOX_HD), BF16), pltpu.VMEM((FOX_PAIR, S, FOX_HD), BF16)],
        compiler_params=_cparams(("parallel", "parallel", "arbitrary")),
        name="fox_attn",
    )(bound, q, k, v, c5, c_tok)


FOX_NORM_SLACK = 1.025


def _fox_mix(x2, B, S, norm_g, w_in, fgate_b, q_g, k_g, w_out):
    q, k, v, c, c_tok = _fox_in(x2, B, S, norm_g, w_in, fgate_b, q_g, k_g, ts=512)
    bound = (FOX_NORM_SLACK * FOX_HD ** 0.5 * jnp.max(jnp.abs(q_g)) * jnp.max(jnp.abs(k_g))).reshape(1, 1).astype(F32)
    o = _fox_attn(q, k, v, c, c_tok, bound, B, S, tq=512)
    return _out_proj1(x2, o.reshape(B * S, FOX_HEADS * FOX_HD), w_out, tm=512)


_PEER_CELLS = tuple((r, c) for r in range(PEER_TOPK) for c in range(PEER_TOPK) if (r + 1) * (c + 1) <= PEER_TOPK)


def _peer_route_kernel(x_ref, g_ref, wqt_ref, keys_ref, xnt_ref, na_ref, e1_ref, rk_ref, e2_ref,
                       s_ref, val_ref, idx_ref, nrow_ref, invz_ref):
    tm = x_ref.shape[0]
    xn = _rms(x_ref[...], g_ref[...])
    xnt_ref[...] = xn.T.astype(xnt_ref.dtype)
    qt = _dot_nt(wqt_ref[...], xn.astype(BF16)).astype(BF16)
    for j in range(2 * PEER_HEADS):
        s_ref[j] = _dot(keys_ref[j], qt[j * PEER_HALF:(j + 1) * PEER_HALF, :])
    kio = lax.broadcasted_iota(jnp.int32, (PEER_NKEYS, tm), 0).astype(F32)
    nkeys = float(PEER_NKEYS)

    def extract(j, carry):
        h = j // 2
        half = j % 2
        work = s_ref[j]
        for r in range(PEER_TOPK):
            m = jnp.max(work, axis=0, keepdims=True)
            idx = jnp.min(jnp.where(work == m, kio, nkeys), axis=0, keepdims=True)
            val_ref[half, r, pl.ds(h, 1), :] = m
            idx_ref[half, r, pl.ds(h, 1), :] = idx
            if r + 1 < PEER_TOPK:
                work = jnp.where(kio == idx, -jnp.inf, work)
        return carry

    lax.fori_loop(0, 2 * PEER_HEADS, extract, 0)

    v1 = [val_ref[0, r] for r in range(PEER_TOPK)]
    v2 = [val_ref[1, c] for c in range(PEER_TOPK)]
    cand = [v1[r] + v2[c] for r, c in _PEER_CELLS]
    flat = [r * PEER_TOPK + c for r, c in _PEER_CELLS]
    big = PEER_TOPK * PEER_TOPK
    sel_val, sel_row = [], []
    for j in range(PEER_TOPK):
        live = [i for i, (r, c) in enumerate(_PEER_CELLS) if (r + 1) * (c + 1) <= j + 1]
        m = functools.reduce(jnp.maximum, [cand[i] for i in live])
        idx = functools.reduce(jnp.minimum, [jnp.where(cand[i] == m, flat[i], big) for i in live])
        sel_val.append(m)
        sel_row.append(idx >> 4)
        if j + 1 < PEER_TOPK:
            for i in live:
                cand[i] = jnp.where(idx == flat[i], -jnp.inf, cand[i])
    z = functools.reduce(jnp.add, [jnp.exp(m - sel_val[0]) for m in sel_val])
    invz_ref[...] = 1.0 / z
    for r in range(PEER_TOPK):
        nrow_ref[r] = functools.reduce(jnp.add, [jnp.where(sr == r, 1.0, 0.0) for sr in sel_row])

    def emit(h, carry):
        row = pl.ds(h, 1)
        na = jnp.zeros((PEER_NKEYS, tm), F32)
        for r in range(PEER_TOPK):
            na = jnp.where(kio == idx_ref[0, r, row, :], nrow_ref[r, row, :], na)
        rk = jnp.full((PEER_NKEYS, tm), float(PEER_TOPK), F32)
        for c in range(PEER_TOPK):
            rk = jnp.where(kio == idx_ref[1, c, row, :], float(c), rk)
        na_ref[h] = na
        rk_ref[h] = rk.astype(rk_ref.dtype)
        e1_ref[h] = jnp.exp(s_ref[2 * h] - val_ref[0, 0, row, :]) * invz_ref[row, :]
        e2_ref[h] = jnp.exp(s_ref[2 * h + 1] - val_ref[1, 0, row, :]).astype(e2_ref.dtype)
        return carry

    lax.fori_loop(0, PEER_HEADS, emit, 0)


def _peer_route(x2, g, wq, keys, tm):
    T, D = x2.shape
    nq = wq.shape[1]
    wqt = wq.astype(BF16).T
    keys2 = keys.reshape(2 * PEER_HEADS, PEER_NKEYS, PEER_HALF).astype(BF16)
    per_key = lambda dt: jax.ShapeDtypeStruct((PEER_HEADS, PEER_NKEYS, T), dt)
    key_spec = pl.BlockSpec((PEER_HEADS, PEER_NKEYS, tm), lambda i: (0, 0, i))
    rank_rows = pltpu.VMEM((2, PEER_TOPK, PEER_HEADS, tm), F32)
    return pl.pallas_call(
        _peer_route_kernel,
        grid=(T // tm,),
        in_specs=[pl.BlockSpec((tm, D), lambda i: (i, 0)), pl.BlockSpec((1, D), lambda i: (0, 0)),
                  pl.BlockSpec((nq, D), lambda i: (0, 0)),
                  pl.BlockSpec(keys2.shape, lambda i: (0, 0, 0))],
        out_specs=[pl.BlockSpec((D, tm), lambda i: (0, i)), key_spec, key_spec, key_spec, key_spec],
        out_shape=[jax.ShapeDtypeStruct((D, T), BF16), per_key(F32), per_key(F32), per_key(BF16), per_key(BF16)],
        scratch_shapes=[pltpu.VMEM((2 * PEER_HEADS, PEER_NKEYS, tm), F32), rank_rows, rank_rows,
                        pltpu.VMEM((PEER_TOPK, PEER_HEADS, tm), F32), pltpu.VMEM((PEER_HEADS, tm), F32)],
        compiler_params=_cparams(("parallel",)),
        name="peer_route",
    )(x2, g.reshape(1, D), wqt, keys2)


PEER_ABLK = 16


def _gelu(x):
    return 0.5 * x * (1.0 + lax.erf(x * (2.0 ** -0.5)))


def _peer_dense_kernel(xnt_ref, u_ref, vt_ref, sel_ref, na_ref, e1_ref, rk_ref, e2_ref, x_ref, o_ref,
                       acc_ref, m_ref, rep_ref):
    j = pl.program_id(1)

    @pl.when(j == 0)
    def _():
        acc_ref[...] = jnp.zeros_like(acc_ref)

    tt = m_ref.shape[1]
    zero = jnp.zeros((BF16_ROWS, tt), BF16)
    npair = PEER_HEADS * PEER_ABLK
    rows = jnp.concatenate([na_ref[...].reshape(npair, tt), e1_ref[...].reshape(npair, tt)], axis=0).astype(BF16)
    rep_ref[...] = _dot(sel_ref[...], rows).astype(BF16)
    for al in range(PEER_ABLK):
        for g in range(PEER_NKEYS // BF16_ROWS):
            bs = slice(g * BF16_ROWS, (g + 1) * BF16_ROWS)
            w = None
            for h in range(PEER_HEADS):
                r_n = (h * PEER_ABLK + al) * BF16_ROWS
                r_e = (npair + h * PEER_ABLK + al) * BF16_ROWS
                e2m = jnp.where(rk_ref[h, bs, :] < rep_ref[r_n:r_n + BF16_ROWS, :], e2_ref[h, bs, :], zero)
                wh = e2m * rep_ref[r_e:r_e + BF16_ROWS, :]
                w = wh if w is None else w + wh
            m_ref[al * PEER_NKEYS + g * BF16_ROWS:al * PEER_NKEYS + (g + 1) * BF16_ROWS, :] = w
    act = _gelu(_dot(u_ref[...], xnt_ref[...]).astype(BF16))
    m_ref[...] = m_ref[...] * act
    acc_ref[...] += _dot(vt_ref[...], m_ref[...])

    @pl.when(j == pl.num_programs(1) - 1)
    def _():
        o_ref[...] = x_ref[...] + acc_ref[...].T


def _peer_dense(x2, xnt, u, v, na, e1, rk, e2, tt):
    T, D = x2.shape
    ne = u.shape[0]
    eb = PEER_ABLK * PEER_NKEYS
    ub = u.astype(BF16)
    vt = v.astype(BF16).T
    nrep = 2 * PEER_HEADS * PEER_ABLK
    sel = (jnp.arange(nrep * BF16_ROWS)[:, None] // BF16_ROWS == jnp.arange(nrep)[None, :]).astype(BF16)
    key_a = pl.BlockSpec((PEER_HEADS, PEER_ABLK, tt), lambda i, j: (0, j, i))
    key_b = pl.BlockSpec((PEER_HEADS, PEER_NKEYS, tt), lambda i, j: (0, 0, i))
    return pl.pallas_call(
        _peer_dense_kernel,
        grid=(T // tt, ne // eb),
        in_specs=[pl.BlockSpec((D, tt), lambda i, j: (0, i)),
                  pl.BlockSpec((eb, D), lambda i, j: (j, 0)),
                  pl.BlockSpec((D, eb), lambda i, j: (0, j)),
                  pl.BlockSpec(sel.shape, lambda i, j: (0, 0)),
                  key_a, key_a, key_b, key_b,
                  pl.BlockSpec((tt, D), lambda i, j: (i, 0))],
        out_specs=pl.BlockSpec((tt, D), lambda i, j: (i, 0)),
        out_shape=jax.ShapeDtypeStruct((T, D), F32),
        scratch_shapes=[pltpu.VMEM((D, tt), F32), pltpu.VMEM((eb, tt), BF16),
                        pltpu.VMEM((nrep * BF16_ROWS, tt), BF16)],
        compiler_params=_cparams(("parallel", "arbitrary")),
        name="peer_dense",
    )(xnt, ub, vt, sel, na, e1, rk, e2, x2)


def _peer(x2, norm_g, wq, keys, u, v):
    T, D = x2.shape
    xnt, na, e1, rk, e2 = _peer_route(x2, norm_g, wq, keys, tm=512)
    return _peer_dense(x2, xnt, u, v, na, e1, rk, e2, tt=512)


def kernel(x, ev_norm_mix, ev_w_in, ev_conv_w, ev_conv_b, ev_conv_ln_g, ev_conv_ln_b, ev_gate_w2, ev_gate_b,
           ev_gla_norm_g, ev_w_out, od_norm_mix, od_w_in, od_fgate_b, od_q_norm_g, od_k_norm_g, od_w_out,
           ffn_norm, peer_wq, peer_keys, peer_u, peer_v):
    B, S, D = x.shape
    x2 = x.reshape(B * S, D)
    x2 = _even_mix(x2, B, S, ev_norm_mix[0], ev_w_in[0], ev_conv_w[0], ev_conv_b[0], ev_conv_ln_g[0],
                   ev_conv_ln_b[0], ev_gate_w2[0], ev_gate_b[0], ev_gla_norm_g[0], ev_w_out[0])
    x2 = _peer(x2, ffn_norm[0], peer_wq[0], peer_keys[0], peer_u[0], peer_v[0])
    x2 = _fox_mix(x2, B, S, od_norm_mix[0], od_w_in[0], od_fgate_b[0], od_q_norm_g[0], od_k_norm_g[0], od_w_out[0])
    x2 = _peer(x2, ffn_norm[1], peer_wq[1], peer_keys[1], peer_u[1], peer_v[1])
    return x2.reshape(B, S, D)
```
